```python
import math
import numpy as np
import jax
import jax.numpy as jnp
from jax import lax

D_MODEL = 1024
BATCH = 8
SEQ = 4096
DEPTH = 4

GRID_W = 64
CTX_LEN = 256
Q_BLOCK = 128
EPS = 1e-6
ROPE_BASE = 10000.0
N_MOD = 6

N_BRANCHES = 4
BRANCH_WIDTH = 256

MLA_HEADS = 4
MLA_Q_RANK = 256
MLA_KV_RANK = 128
MLA_NOPE = 64
MLA_ROPE = 32
MLA_V = 64

NA_HEADS = 4
NA_HEAD_DIM = 64
NA_WIN_ROWS = 8
NA_WIN_COLS = 16

DIFF_HEADS = 4
DIFF_HALF = 32
DIFF_V = 2 * DIFF_HALF

GQA_Q_HEADS = 4
GQA_KV_HEADS = 2
GQA_HEAD_DIM = 64

PEER_HEADS = 8
PEER_NKEYS = 128
PEER_EXPERTS = PEER_NKEYS * PEER_NKEYS
PEER_QDIM = 256
PEER_TOPK = 16
PEER_CHUNK = 128

IN_SIZES = (
    MLA_Q_RANK, MLA_KV_RANK, MLA_ROPE,
    NA_HEADS * NA_HEAD_DIM, NA_HEADS * NA_HEAD_DIM, NA_HEADS * NA_HEAD_DIM,
    DIFF_HEADS * 2 * DIFF_HALF, DIFF_HEADS * 2 * DIFF_HALF, DIFF_HEADS * DIFF_V,
    GQA_Q_HEADS * GQA_HEAD_DIM, GQA_KV_HEADS * GQA_HEAD_DIM, GQA_KV_HEADS * GQA_HEAD_DIM,
)
IN_COLS = sum(IN_SIZES)

kernel_name = 'hybrid_latent_peer_trunk'


def rmsnorm(x, g):
    x32 = x.astype(jnp.float32)
    y = x32 * lax.rsqrt(jnp.mean(x32 * x32, axis=-1, keepdims=True) + EPS)
    return (y * g.astype(jnp.float32)).astype(x.dtype)


def modulate(x, shift, scale):
    return x * (1 + scale) + shift


def split_heads(t, n_heads):
    b, l, _ = t.shape
    return t.reshape(b, l, n_heads, -1).transpose(0, 2, 1, 3)


def merge_heads(t):
    b, h, l, d = t.shape
    return t.transpose(0, 2, 1, 3).reshape(b, l, h * d)


def split_cols(p):
    return jnp.split(p, np.cumsum(IN_SIZES)[:-1].tolist(), axis=-1)


def rope_1d(t, pos):
    d = t.shape[-1]
    inv_freq = jnp.power(ROPE_BASE, -jnp.arange(0, d, 2, dtype=jnp.float32) / d)
    ang = pos[:, None] * inv_freq[None, :]
    cos = jnp.cos(ang).astype(t.dtype)
    sin = jnp.sin(ang).astype(t.dtype)
    t1, t2 = t[..., : d // 2], t[..., d // 2:]
    return jnp.concatenate([t1 * cos - t2 * sin, t1 * sin + t2 * cos], axis=-1)


def axial_rope(t, rows, cols):
    h = t.shape[-1] // 2
    return jnp.concatenate([rope_1d(t[..., :h], rows), rope_1d(t[..., h:], cols)], axis=-1)


def sweep_query_blocks(f, *qs):
    l = qs[0].shape[-2]
    nb = l // Q_BLOCK

    def to_blocks(a):
        a = a.reshape(a.shape[:-2] + (nb, Q_BLOCK, a.shape[-1]))
        return jnp.moveaxis(a, -3, 0)

    out = lax.map(lambda blk: f(*blk), tuple(to_blocks(a) for a in qs))
    out = jnp.moveaxis(out, 0, -3)
    return out.reshape(out.shape[:-3] + (l, out.shape[-1]))


def attend(q, k, v, scale):
    s = jnp.einsum('bkgqd,bksd->bkgqs', q, k).astype(jnp.float32) * scale
    p = jax.nn.softmax(s, axis=-1).astype(v.dtype)
    return jnp.einsum('bkgqs,bksd->bkgqd', p, v)


def attend_latent_and_context(q_lat, k_lat, v_lat, q_ctx, k_ctx, v_ctx, scale):
    k_all = jnp.concatenate([k_lat, k_ctx], axis=2)
    v_all = jnp.concatenate([v_lat, v_ctx], axis=2)
    o_lat = sweep_query_blocks(lambda qb: attend(qb, k_all, v_all, scale), q_lat)
    o_ctx = None if q_ctx is None else attend(q_ctx, k_ctx, v_ctx, scale)
    return o_lat, o_ctx


def mla_mixer(pl, pc, rows, cols, q_norm, kv_norm, w_uq, w_ukv, with_ctx):
    def queries(q_c, rotate):
        q = split_heads(rmsnorm(q_c, q_norm) @ w_uq, MLA_HEADS)
        q_rope = q[..., MLA_NOPE:]
        if rotate:
            q_rope = axial_rope(q_rope, rows, cols)
        return jnp.concatenate([q[..., :MLA_NOPE], q_rope], axis=-1)[:, :, None]

    def keys_values(kv_c, k_r, rotate):
        kv = split_heads(rmsnorm(kv_c, kv_norm) @ w_ukv, MLA_HEADS)
        k_nope, v = kv[..., :MLA_NOPE], kv[..., MLA_NOPE:]
        k_rope = k_r[:, None]
        if rotate:
            k_rope = axial_rope(k_rope, rows, cols)
        k_rope = jnp.broadcast_to(k_rope, k_nope.shape[:-1] + (MLA_ROPE,))
        return jnp.concatenate([k_nope, k_rope], axis=-1), v

    q_lat = queries(pl[0], True)
    k_lat, v_lat = keys_values(pl[1], pl[2], True)
    k_ctx, v_ctx = keys_values(pc[1], pc[2], False)
    q_ctx = queries(pc[0], False) if with_ctx else None
    scale = (MLA_NOPE + MLA_ROPE) ** -0.5
    o_lat, o_ctx = attend_latent_and_context(q_lat, k_lat, v_lat, q_ctx, k_ctx, v_ctx, scale)
    o_ctx = merge_heads(o_ctx[:, :, 0]) if with_ctx else None
    return merge_heads(o_lat[:, :, 0]), o_ctx


def na_mixer(pl, pc, rpb, with_ctx):
    q, k, v = [split_heads(t, NA_HEADS) for t in pl]
    k_ctx, v_ctx = split_heads(pc[1], NA_HEADS), split_heads(pc[2], NA_HEADS)
    b, h, l, d = q.shape
    n_rows = l // GRID_W
    kr = min(NA_WIN_ROWS, n_rows)
    nc = NA_WIN_COLS
    scale = d ** -0.5
    q_g = q.reshape(b, h, n_rows, GRID_W, d)
    k_g = k.reshape(b, h, n_rows, GRID_W, d)
    v_g = v.reshape(b, h, n_rows, GRID_W, d)
    col_q = np.arange(GRID_W)
    col_start = np.clip(col_q - nc // 2, 0, GRID_W - nc)
    col_idx = col_start[:, None] + np.arange(nc)[None, :]
    col_off = col_idx - col_q[:, None] + (NA_WIN_COLS - 1)

    def row_block(args):
        r, q_r = args
        rs = jnp.clip(r - kr // 2, 0, n_rows - kr)
        k_rows = lax.dynamic_slice_in_dim(k_g, rs, kr, axis=2)
        v_rows = lax.dynamic_slice_in_dim(v_g, rs, kr, axis=2)
        k_win = k_rows[:, :, :, col_idx]
        v_win = v_rows[:, :, :, col_idx]
        row_off = rs + jnp.arange(kr) - r + (NA_WIN_ROWS - 1)
        bias = rpb[:, row_off[None, :, None], col_off[:, None, :]]
        s_loc = jnp.einsum('bhcd,bhicjd->bhcij', q_r, k_win).astype(jnp.float32) * scale
        s_loc = s_loc + bias.astype(jnp.float32)
        s_ctx = jnp.einsum('bhcd,bhsd->bhcs', q_r, k_ctx).astype(jnp.float32) * scale
        s = jnp.concatenate([s_loc.reshape(b, h, GRID_W, kr * nc), s_ctx], axis=-1)
        p = jax.nn.softmax(s, axis=-1).astype(v.dtype)
        p_loc = p[..., : kr * nc].reshape(b, h, GRID_W, kr, nc)
        p_ctx = p[..., kr * nc:]
        return (jnp.einsum('bhcij,bhicjd->bhcd', p_loc, v_win)
                + jnp.einsum('bhcs,bhsd->bhcd', p_ctx, v_ctx))

    o = lax.map(row_block, (jnp.arange(n_rows), jnp.moveaxis(q_g, 2, 0)))
    o_lat = merge_heads(jnp.moveaxis(o, 0, 2).reshape(b, h, l, d))
    o_ctx = None
    if with_ctx:
        q_ctx = split_heads(pc[0], NA_HEADS)[:, :, None]
        o_ctx = merge_heads(attend(q_ctx, k_ctx, v_ctx, scale)[:, :, 0])
    return o_lat, o_ctx


def diff_attend(q1, q2, k1, k2, v, lam, scale):
    s1 = jnp.einsum('bhqd,bhsd->bhqs', q1, k1).astype(jnp.float32) * scale
    s2 = jnp.einsum('bhqd,bhsd->bhqs', q2, k2).astype(jnp.float32) * scale
    p = jax.nn.softmax(s1, axis=-1) - lam * jax.nn.softmax(s2, axis=-1)
    return jnp.einsum('bhqs,bhsd->bhqd', p.astype(v.dtype), v)


def diff_mixer(pl, pc, rows, cols, lq1, lk1, lq2, lk2, subln, lam_init, with_ctx):
    q, k, v = [split_heads(t, DIFF_HEADS) for t in pl]
    q1 = axial_rope(q[..., :DIFF_HALF], rows, cols)
    q2 = axial_rope(q[..., DIFF_HALF:], rows, cols)
    k1 = axial_rope(k[..., :DIFF_HALF], rows, cols)
    k2 = axial_rope(k[..., DIFF_HALF:], rows, cols)
    kc = split_heads(pc[1], DIFF_HEADS)
    vc = split_heads(pc[2], DIFF_HEADS)
    kc1, kc2 = kc[..., :DIFF_HALF], kc[..., DIFF_HALF:]
    f32 = jnp.float32
    lam = (jnp.exp(jnp.sum(lq1.astype(f32) * lk1.astype(f32)))
           - jnp.exp(jnp.sum(lq2.astype(f32) * lk2.astype(f32))) + lam_init)
    scale = DIFF_HALF ** -0.5
    k1_all = jnp.concatenate([k1, kc1], axis=2)
    k2_all = jnp.concatenate([k2, kc2], axis=2)
    v_all = jnp.concatenate([v, vc], axis=2)
    o_lat = sweep_query_blocks(
        lambda a, b_: diff_attend(a, b_, k1_all, k2_all, v_all, lam, scale), q1, q2)

    def finish(o):
        return merge_heads(rmsnorm(o, subln) * (1 - lam_init))

    o_ctx = None
    if with_ctx:
        qc = split_heads(pc[0], DIFF_HEADS)
        o_ctx = finish(diff_attend(qc[..., :DIFF_HALF], qc[..., DIFF_HALF:], kc1, kc2, vc, lam, scale))
    return finish(o_lat), o_ctx


def gqa_mixer(pl, pc, rows, cols, q_norm, k_norm, with_ctx):
    group = GQA_Q_HEADS // GQA_KV_HEADS

    def group_q(q):
        b, h, l, d = q.shape
        return q.reshape(b, GQA_KV_HEADS, group, l, d)

    def ungroup(o):
        b, hk, g, l, d = o.shape
        return merge_heads(o.reshape(b, hk * g, l, d))

    q = axial_rope(rmsnorm(split_heads(pl[0], GQA_Q_HEADS), q_norm), rows, cols)
    k = axial_rope(rmsnorm(split_heads(pl[1], GQA_KV_HEADS), k_norm), rows, cols)
    v = split_heads(pl[2], GQA_KV_HEADS)
    k_ctx = rmsnorm(split_heads(pc[1], GQA_KV_HEADS), k_norm)
    v_ctx = split_heads(pc[2], GQA_KV_HEADS)
    q_ctx = group_q(rmsnorm(split_heads(pc[0], GQA_Q_HEADS), q_norm)) if with_ctx else None
    o_lat, o_ctx = attend_latent_and_context(group_q(q), k, v, q_ctx, k_ctx, v_ctx,
                                             GQA_HEAD_DIM ** -0.5)
    return ungroup(o_lat), (ungroup(o_ctx) if with_ctx else None)


def merge_branches(h, outs, w_branch, w_gate, b_gate, w_out):
    terms = [jax.nn.sigmoid(h @ w_gate[i] + b_gate[i]) * (o @ w_branch[i]) for i, o in enumerate(outs)]
    merged = terms[0]
    for t in terms[1:]:
        merged = merged + t
    return merged @ w_out


def peer_ffn(h, w_q, subkeys, u, v):
    b, l, d = h.shape
    tokens = h.reshape(-1, PEER_CHUNK, d)
    kk = PEER_TOPK

    def chunk(x_t):
        t = x_t.shape[0]
        q = (x_t @ w_q).reshape(t, PEER_HEADS, 2, PEER_QDIM // 2)
        s = jnp.einsum('thpd,hpnd->thpn', q, subkeys).astype(jnp.float32)
        s_top, i_top = lax.top_k(s, kk)
        cand = (s_top[:, :, 0, :, None] + s_top[:, :, 1, None, :]).reshape(t, PEER_HEADS, kk * kk)
        cand_idx = (i_top[:, :, 0, :, None] * PEER_NKEYS
                    + i_top[:, :, 1, None, :]).reshape(t, PEER_HEADS, kk * kk)
        best, pos = lax.top_k(cand, kk)
        experts = jnp.take_along_axis(cand_idx, pos, axis=-1)
        g = jax.nn.softmax(best, axis=-1).astype(x_t.dtype)
        u_sel = jnp.take(u, experts, axis=0)
        v_sel = jnp.take(v, experts, axis=0)
        a = jax.nn.gelu(jnp.einsum('td,thkd->thk', x_t, u_sel), approximate=False)
        return jnp.einsum('thk,thkd->td', g * a, v_sel)

    return lax.map(chunk, tokens).reshape(b, l, d)


def setup_inputs(seed: int = 0) -> dict:
    key = jax.random.key(seed)
    ks = iter(jax.random.split(key, 40))
    D = D_MODEL

    def nrm(shape, scale):
        return jax.random.normal(next(ks), shape, jnp.float32) * scale

    def gain(shape):
        return 1.0 + nrm(shape, 0.02)

    return {
        'x': nrm((BATCH, SEQ, D), 1.0),
        'c': nrm((BATCH, D), 1.0),
        'ctx': nrm((BATCH, CTX_LEN, D), 1.0),
        'c_ctx': nrm((D,), 1.0),
        'w_mod': nrm((DEPTH, D, N_MOD * D), 0.5 * D ** -0.5),
        'b_mod': nrm((DEPTH, N_MOD * D), 0.02),
        'norm_mix': gain((DEPTH, D)),
        'norm_ffn': gain((DEPTH, D)),
        'w_in': nrm((DEPTH, D, IN_COLS), D ** -0.5),
        'mla_q_norm': gain((DEPTH, MLA_Q_RANK)),
        'mla_kv_norm': gain((DEPTH, MLA_KV_RANK)),
        'mla_w_uq': nrm((DEPTH, MLA_Q_RANK, MLA_HEADS * (MLA_NOPE + MLA_ROPE)), MLA_Q_RANK ** -0.5),
        'mla_w_ukv': nrm((DEPTH, MLA_KV_RANK, MLA_HEADS * (MLA_NOPE + MLA_V)), MLA_KV_RANK ** -0.5),
        'na_rpb': nrm((DEPTH, NA_HEADS, 2 * NA_WIN_ROWS - 1, 2 * NA_WIN_COLS - 1), 0.5),
        'diff_lam_q1': nrm((DEPTH, DIFF_HALF), 0.1),
        'diff_lam_k1': nrm((DEPTH, DIFF_HALF), 0.1),
        'diff_lam_q2': nrm((DEPTH, DIFF_HALF), 0.1),
        'diff_lam_k2': nrm((DEPTH, DIFF_HALF), 0.1),
        'diff_subln': gain((DEPTH, DIFF_V)),
        'gqa_q_norm': gain((DEPTH, GQA_HEAD_DIM)),
        'gqa_k_norm': gain((DEPTH, GQA_HEAD_DIM)),
        'w_branch': nrm((DEPTH, N_BRANCHES, BRANCH_WIDTH, D), BRANCH_WIDTH ** -0.5),
        'w_gate': nrm((DEPTH, N_BRANCHES, D, D), D ** -0.5),
        'b_gate': nrm((DEPTH, N_BRANCHES, D), 0.02),
        'w_out': nrm((DEPTH, D, D), D ** -0.5),
        'peer_w_q': nrm((DEPTH, D, PEER_HEADS * PEER_QDIM), D ** -0.5),
        'peer_subkeys': nrm((DEPTH, PEER_HEADS, 2, PEER_NKEYS, PEER_QDIM // 2), (PEER_QDIM // 2) ** -0.5),
        'peer_u': nrm((DEPTH, PEER_EXPERTS, D), D ** -0.5),
        'peer_v': nrm((DEPTH, PEER_EXPERTS, D), 0.3),
        'final_norm': gain((D,)),
    }


def reference(x, c, ctx, c_ctx, w_mod, b_mod, norm_mix, norm_ffn, w_in,
              mla_q_norm, mla_kv_norm, mla_w_uq, mla_w_ukv, na_rpb,
              diff_lam_q1, diff_lam_k1, diff_lam_q2, diff_lam_k2, diff_subln,
              gqa_q_norm, gqa_k_norm, w_branch, w_gate, b_gate, w_out,
              peer_w_q, peer_subkeys, peer_u, peer_v, final_norm):
    l_lat = x.shape[1]
    pos = jnp.arange(l_lat)
    rows = (pos // GRID_W).astype(jnp.float32)
    cols = (pos % GRID_W).astype(jnp.float32)
    for l in range(DEPTH):
        with_ctx = l < DEPTH - 1
        mod_lat = (jax.nn.silu(c) @ w_mod[l] + b_mod[l])[:, None, :]
        mod_ctx = jax.nn.silu(c_ctx) @ w_mod[l] + b_mod[l]
        sh1, sc1, g1, sh2, sc2, g2 = jnp.split(mod_lat, N_MOD, axis=-1)
        csh1, csc1, cg1, csh2, csc2, cg2 = jnp.split(mod_ctx, N_MOD, axis=-1)

        h = modulate(rmsnorm(x, norm_mix[l]), sh1, sc1)
        hc = modulate(rmsnorm(ctx, norm_mix[l]), csh1, csc1)
        pl = split_cols(h @ w_in[l])
        pc = split_cols(hc @ w_in[l])
        outs = [
            mla_mixer(pl[0:3], pc[0:3], rows, cols, mla_q_norm[l], mla_kv_norm[l],
                      mla_w_uq[l], mla_w_ukv[l], with_ctx),
            na_mixer(pl[3:6], pc[3:6], na_rpb[l], with_ctx),
            diff_mixer(pl[6:9], pc[6:9], rows, cols, diff_lam_q1[l], diff_lam_k1[l],
                       diff_lam_q2[l], diff_lam_k2[l], diff_subln[l],
                       0.8 - 0.6 * math.exp(-0.3 * l), with_ctx),
            gqa_mixer(pl[9:12], pc[9:12], rows, cols, gqa_q_norm[l], gqa_k_norm[l], with_ctx),
        ]
        x = x + g1 * merge_branches(h, [o[0] for o in outs], w_branch[l], w_gate[l], b_gate[l], w_out[l])

        h = modulate(rmsnorm(x, norm_ffn[l]), sh2, sc2)
        x = x + g2 * peer_ffn(h, peer_w_q[l], peer_subkeys[l], peer_u[l], peer_v[l])

        if with_ctx:
            ctx = ctx + cg1 * merge_branches(hc, [o[1] for o in outs], w_branch[l], w_gate[l],
                                             b_gate[l], w_out[l])
            hc = modulate(rmsnorm(ctx, norm_ffn[l]), csh2, csc2)
            ctx = ctx + cg2 * peer_ffn(hc, peer_w_q[l], peer_subkeys[l], peer_u[l], peer_v[l])
    return rmsnorm(x, final_norm)
```

```python
import functools
import math

import numpy as np
import jax
import jax.numpy as jnp
from jax import lax
from jax.experimental import pallas as pl
from jax.experimental.pallas import tpu as pltpu

D_MODEL = 1024
GRID_W = 64
EPS = 1e-6
ROPE_BASE = 10000.0
N_MOD = 6

MLA_HEADS = 4
MLA_Q_RANK = 256
MLA_KV_RANK = 128
MLA_NOPE = 64
MLA_ROPE = 32
MLA_V = 64
MLA_QK = MLA_NOPE + MLA_ROPE

NA_HEADS = 4
NA_HEAD_DIM = 64
NA_WIN_ROWS = 8
NA_WIN_COLS = 16

DIFF_HEADS = 4
DIFF_HALF = 32
DIFF_V = 2 * DIFF_HALF

GQA_Q_HEADS = 4
GQA_KV_HEADS = 2
GQA_HEAD_DIM = 64

PEER_HEADS = 8
PEER_NKEYS = 128
PEER_EXPERTS = PEER_NKEYS * PEER_NKEYS
PEER_QDIM = 256
PEER_TOPK = 16
PEER_PAIRS = PEER_HEADS * PEER_TOPK

LANES = 128
VMEM_LIMIT_BYTES = 56 * 1024 * 1024

COL_QC = 0
COL_KVC = 256
COL_NA = 384
COL_DIFF = 1152
COL_GQA = 1920
COL_KR = 2432
IN_COLS_PAD = 2560
MOD_ROWS = 16

TOK_BLOCK = 256
PEER_TB = 32
PACK_ROWS = 4
GATHER_UNROLL = 32

_NT = (((1,), (1,)), ((), ()))
_F32 = jnp.float32
_BF16 = jnp.bfloat16


def _cparams(sem):
    return pltpu.CompilerParams(dimension_semantics=sem, vmem_limit_bytes=VMEM_LIMIT_BYTES)


def _split_bf16(a):
    hi = a.astype(_BF16)
    lo = (a - hi.astype(_F32)).astype(_BF16)
    return hi, lo


def _dot(a, b):
    return jnp.dot(a, b, preferred_element_type=_F32)


def _norm_mod(x, g, shift, scale):
    y = x * lax.rsqrt(jnp.mean(x * x, axis=-1, keepdims=True) + EPS)
    return (y * g) * (1.0 + scale) + shift


def _mod_kernel(c_ref, w_ref, b_ref, o_ref):
    c = c_ref[...]
    s = c * (1.0 / (1.0 + jnp.exp(-c)))
    s_hi, s_lo = _split_bf16(s)
    w_hi, w_lo = _split_bf16(w_ref[0])
    o_ref[0] = _dot(s_hi, w_hi) + _dot(s_hi, w_lo) + _dot(s_lo, w_hi) + b_ref[0]


def _mod_all(cond, w_mod, b_mod):
    depth, d, n = w_mod.shape
    tn = 1536
    return pl.pallas_call(
        _mod_kernel,
        grid=(depth, n // tn),
        in_specs=[
            pl.BlockSpec((MOD_ROWS, d), lambda l, j: (0, 0)),
            pl.BlockSpec((1, d, tn), lambda l, j: (l, 0, j)),
            pl.BlockSpec((1, 1, tn), lambda l, j: (l, 0, j)),
        ],
        out_specs=pl.BlockSpec((1, MOD_ROWS, tn), lambda l, j: (l, 0, j)),
        out_shape=jax.ShapeDtypeStruct((depth, MOD_ROWS, n), _F32),
        compiler_params=_cparams(("arbitrary", "arbitrary")),
        name="adaln_mod",
    )(cond, w_mod, b_mod.reshape(depth, 1, n))


def _rope(t, cos, sa, sb, dist):
    w = t.shape[-1]
    return t * cos + pltpu.roll(t, w - dist, 1) * sa + pltpu.roll(t, dist, 1) * sb


def _seg_mean_sq(t, seg_ref, seg_len):
    hi, lo = _split_bf16(t * t)
    seg = seg_ref[...]
    return (_dot(hi, seg) + _dot(lo, seg)) * (1.0 / seg_len)


def _proj_kernel(x_ref, sh_ref, sc_ref, g_ref, win_ref, qn_ref, kvn_ref, wuq_ref, wukv_ref,
                 gq_ref, gk_ref, seg_ref, cq_ref, saq_ref, sbq_ref, c32_ref, sa32_ref, sb32_ref,
                 c64_ref, sa64_ref, sb64_ref,
                 mq_ref, mkv_ref, mkr_ref, na_ref, df_ref, gqa_ref):
    h = _norm_mod(x_ref[...], g_ref[...], sh_ref[0], sc_ref[0])
    p = _dot(h.astype(_BF16), win_ref[...])

    qc = p[:, COL_QC:COL_QC + MLA_Q_RANK]
    qn = qc * lax.rsqrt(jnp.mean(qc * qc, axis=-1, keepdims=True) + EPS) * qn_ref[...]
    q = _dot(qn.astype(_BF16), wuq_ref[...])
    mq_ref[...] = _rope(q, cq_ref[...], saq_ref[...], sbq_ref[...], MLA_ROPE // 4).astype(_BF16)
    kvc = p[:, COL_KVC:COL_KVC + MLA_KV_RANK]
    kvn = kvc * lax.rsqrt(jnp.mean(kvc * kvc, axis=-1, keepdims=True) + EPS) * kvn_ref[...]
    mkv_ref[...] = _dot(kvn.astype(_BF16), wukv_ref[...]).astype(_BF16)
    kr = p[:, COL_KR:COL_KR + LANES]
    mkr_ref[...] = _rope(kr, c32_ref[:, :LANES], sa32_ref[:, :LANES], sb32_ref[:, :LANES],
                         MLA_ROPE // 4).astype(_BF16)

    na_ref[...] = p[:, COL_NA:COL_NA + 768].astype(_BF16)

    dq = p[:, COL_DIFF:COL_DIFF + 256]
    dk = p[:, COL_DIFF + 256:COL_DIFF + 512]
    c32, sa32, sb32 = c32_ref[...], sa32_ref[...], sb32_ref[...]
    df_ref[:, 0:256] = _rope(dq, c32, sa32, sb32, DIFF_HALF // 4).astype(_BF16)
    df_ref[:, 256:512] = _rope(dk, c32, sa32, sb32, DIFF_HALF // 4).astype(_BF16)
    df_ref[:, 512:768] = p[:, COL_DIFF + 512:COL_DIFF + 768].astype(_BF16)

    gq = p[:, COL_GQA:COL_GQA + 256]
    gk = p[:, COL_GQA + 256:COL_GQA + 384]
    gqn = gq * lax.rsqrt(_seg_mean_sq(gq, seg_ref, GQA_HEAD_DIM) + EPS) * gq_ref[...]
    gk2 = jnp.concatenate([gk, gk], axis=-1)
    gkn = gk2 * lax.rsqrt(_seg_mean_sq(gk2, seg_ref, GQA_HEAD_DIM) + EPS) * gk_ref[...]
    c64, sa64, sb64 = c64_ref[...], sa64_ref[...], sb64_ref[...]
    gqa_ref[:, 0:256] = _rope(gqn, c64, sa64, sb64, GQA_HEAD_DIM // 4).astype(_BF16)
    gqa_ref[:, 256:384] = _rope(gkn, c64, sa64, sb64, GQA_HEAD_DIM // 4)[:, :128].astype(_BF16)
    gqa_ref[:, 384:512] = p[:, COL_GQA + 384:COL_GQA + 512].astype(_BF16)


def _proj_prep(xa, mod_l, g_norm, win, qn, kvn, wuq, wukv, gq, gk, seg, tabs, n_lat_blocks, blocks_per_batch,
               n_batch, pos_blocks):
    nt, d = xa.shape
    tb = TOK_BLOCK

    def mrow(i):
        return jnp.minimum(i // blocks_per_batch, n_batch)

    def trow(i):
        return jnp.where(i < n_lat_blocks, i % pos_blocks, pos_blocks)

    full = lambda a: pl.BlockSpec(a.shape, lambda i: (0,) * a.ndim)
    tab_specs = [pl.BlockSpec((tb, t.shape[1]), lambda i: (trow(i), 0)) for t in tabs]
    widths = (384, 512, 128, 768, 768, 512)
    return pl.pallas_call(
        _proj_kernel,
        grid=(nt // tb,),
        in_specs=[
            pl.BlockSpec((tb, d), lambda i: (i, 0)),
            pl.BlockSpec((1, 1, d), lambda i: (mrow(i), 0, 0)),
            pl.BlockSpec((1, 1, d), lambda i: (mrow(i), 0, 1)),
            full(g_norm), full(win), full(qn), full(kvn), full(wuq), full(wukv), full(gq), full(gk), full(seg),
        ] + tab_specs,
        out_specs=[pl.BlockSpec((tb, w), lambda i: (i, 0)) for w in widths],
        out_shape=[jax.ShapeDtypeStruct((nt, w), _BF16) for w in widths],
        compiler_params=_cparams(("arbitrary",)),
        name="mix_in_proj",
    )(xa, mod_l, mod_l, g_norm, win, qn, kvn, wuq, wukv, gq, gk, seg, *tabs)


def _attn_kernel(q_ref, k_ref, v_ref, o_ref, *, scale):
    s = lax.dot_general(q_ref[0, 0], k_ref[0, 0], _NT, preferred_element_type=_F32) * scale
    m = jnp.max(s, axis=-1, keepdims=True)
    e = jnp.exp(s - m)
    l = jnp.sum(e, axis=-1, keepdims=True)
    o = _dot(e.astype(_BF16), v_ref[0, 0])
    o_ref[0, 0] = (o / l).astype(o_ref.dtype)


def _attention(q, k, v, scale, tq):
    b, hq, lq, dq = q.shape
    _, hk, s, dv = v.shape
    group = hq // hk
    return pl.pallas_call(
        functools.partial(_attn_kernel, scale=scale),
        grid=(b, hq, lq // tq),
        in_specs=[
            pl.BlockSpec((1, 1, tq, dq), lambda bi, h, i: (bi, h, i, 0)),
            pl.BlockSpec((1, 1, s, dq), lambda bi, h, i: (bi, h // group, 0, 0)),
            pl.BlockSpec((1, 1, s, dv), lambda bi, h, i: (bi, h // group, 0, 0)),
        ],
        out_specs=pl.BlockSpec((1, 1, tq, dv), lambda bi, h, i: (bi, h, i, 0)),
        out_shape=jax.ShapeDtypeStruct((b, hq, lq, dv), _BF16),
        compiler_params=_cparams(("arbitrary", "arbitrary", "arbitrary")),
        name="softmax_attention",
    )(q, k, v)


def _diff_kernel(q_ref, k_ref, v_ref, lam_ref, sub_ref, o_ref, *, scale, lam_init):
    lam = (jnp.exp(jnp.sum(lam_ref[0:1, :] * lam_ref[1:2, :], axis=-1, keepdims=True))
           - jnp.exp(jnp.sum(lam_ref[2:3, :] * lam_ref[3:4, :], axis=-1, keepdims=True)) + lam_init)
    q = q_ref[0, 0]
    k = k_ref[0, 0]
    v = v_ref[0, 0]
    first = lax.broadcasted_iota(jnp.int32, q.shape, 1) < DIFF_HALF
    zero = jnp.zeros_like(q)

    def one_map(qm):
        s = lax.dot_general(qm, k, _NT, preferred_element_type=_F32) * scale
        m = jnp.max(s, axis=-1, keepdims=True)
        e = jnp.exp(s - m)
        l = jnp.sum(e, axis=-1, keepdims=True)
        return _dot(e.astype(_BF16), v) / l

    o = one_map(jnp.where(first, q, zero)) - lam * one_map(jnp.where(first, zero, q))
    y = o * lax.rsqrt(jnp.mean(o * o, axis=-1, keepdims=True) + EPS) * sub_ref[...]
    o_ref[0, 0] = (y * (1.0 - lam_init)).astype(o_ref.dtype)


def _diff_attention(q, k, v, lam_vecs, subln, lam_init, tq):
    b, h, lq, dq = q.shape
    s, dv = v.shape[2], v.shape[3]
    return pl.pallas_call(
        functools.partial(_diff_kernel, scale=DIFF_HALF ** -0.5, lam_init=lam_init),
        grid=(b, h, lq // tq),
        in_specs=[
            pl.BlockSpec((1, 1, tq, dq), lambda bi, hh, i: (bi, hh, i, 0)),
            pl.BlockSpec((1, 1, s, dq), lambda bi, hh, i: (bi, hh, 0, 0)),
            pl.BlockSpec((1, 1, s, dv), lambda bi, hh, i: (bi, hh, 0, 0)),
            pl.BlockSpec(lam_vecs.shape, lambda bi, hh, i: (0, 0)),
            pl.BlockSpec(subln.shape, lambda bi, hh, i: (0, 0)),
        ],
        out_specs=pl.BlockSpec((1, 1, tq, dv), lambda bi, hh, i: (bi, hh, i, 0)),
        out_shape=jax.ShapeDtypeStruct((b, h, lq, dv), _BF16),
        compiler_params=_cparams(("arbitrary", "arbitrary", "arbitrary")),
        name="diff_attention",
    )(q, k, v, lam_vecs, subln)


def _na_kernel(q_ref, k_ref, v_ref, kc_ref, vc_ref, bias_ref, o_ref, *, scale, n_rows, kr):
    r = pl.program_id(2)
    rs = jnp.clip(r - kr // 2, 0, n_rows - kr)
    start = pl.multiple_of(rs * GRID_W, GRID_W)
    q = q_ref[0, 0]
    k_win = k_ref[0, 0, pl.ds(start, kr * GRID_W), :]
    v_win = v_ref[0, 0, pl.ds(start, kr * GRID_W), :]
    s_loc = lax.dot_general(q, k_win, _NT, preferred_element_type=_F32) * scale + bias_ref[0, r - rs]
    s_ctx = lax.dot_general(q, kc_ref[0, 0], _NT, preferred_element_type=_F32) * scale
    m = jnp.maximum(jnp.max(s_loc, axis=-1, keepdims=True), jnp.max(s_ctx, axis=-1, keepdims=True))
    e_loc = jnp.exp(s_loc - m)
    e_ctx = jnp.exp(s_ctx - m)
    l = jnp.sum(e_loc, axis=-1, keepdims=True) + jnp.sum(e_ctx, axis=-1, keepdims=True)
    o = _dot(e_loc.astype(_BF16), v_win) + _dot(e_ctx.astype(_BF16), vc_ref[0, 0])
    o_ref[0, 0] = (o / l).astype(o_ref.dtype)


def _na_attention(q, k, v, kc, vc, bias):
    b, h, l, d = q.shape
    c = kc.shape[2]
    n_rows = l // GRID_W
    kr = min(NA_WIN_ROWS, n_rows)
    return pl.pallas_call(
        functools.partial(_na_kernel, scale=d ** -0.5, n_rows=n_rows, kr=kr),
        grid=(b, h, n_rows),
        in_specs=[
            pl.BlockSpec((1, 1, GRID_W, d), lambda bi, hh, r: (bi, hh, r, 0)),
            pl.BlockSpec((1, 1, l, d), lambda bi, hh, r: (bi, hh, 0, 0)),
            pl.BlockSpec((1, 1, l, d), lambda bi, hh, r: (bi, hh, 0, 0)),
            pl.BlockSpec((1, 1, c, d), lambda bi, hh, r: (bi, hh, 0, 0)),
            pl.BlockSpec((1, 1, c, d), lambda bi, hh, r: (bi, hh, 0, 0)),
            pl.BlockSpec((1,) + bias.shape[1:], lambda bi, hh, r: (hh, 0, 0, 0)),
        ],
        out_specs=pl.BlockSpec((1, 1, GRID_W, d), lambda bi, hh, r: (bi, hh, r, 0)),
        out_shape=jax.ShapeDtypeStruct((b, h, l, d), _BF16),
        compiler_params=_cparams(("arbitrary", "arbitrary", "arbitrary")),
        name="neighbourhood_attention",
    )(q, k, v, kc, vc, bias)


def _na_bias_table(rpb, n_rows):
    kr = min(NA_WIN_ROWS, n_rows)
    nc = NA_WIN_COLS
    col_q = np.arange(GRID_W)
    col_start = np.clip(col_q - nc // 2, 0, GRID_W - nc)
    key_c = np.arange(GRID_W)
    inside = (key_c[None, :] >= col_start[:, None]) & (key_c[None, :] < col_start[:, None] + nc)
    col_off = np.clip(key_c[None, :] - col_q[:, None] + (NA_WIN_COLS - 1), 0, 2 * NA_WIN_COLS - 2)
    dd = np.arange(kr)[:, None]
    ii = np.arange(kr)[None, :]
    row_off = ii - dd + (NA_WIN_ROWS - 1)
    tab = rpb[:, row_off[:, None, :, None], col_off[None, :, None, :]]
    tab = jnp.where(inside[None, None, :, None, :], tab, -1e30)
    return tab.reshape(rpb.shape[0], kr, GRID_W, kr * GRID_W).astype(_F32)


def _merge_kernel(x_ref, sh_ref, sc_ref, g1_ref, g_ref, o_ref, wg_ref, bg_ref, wb_ref, wo_ref, y_ref):
    x = x_ref[...]
    hb = _norm_mod(x, g_ref[...], sh_ref[0], sc_ref[0]).astype(_BF16)
    o = o_ref[...]
    merged = None
    for i in range(4):
        gate = 1.0 / (1.0 + jnp.exp(-(_dot(hb, wg_ref[i]) + bg_ref[i])))
        term = gate * _dot(o[:, 256 * i:256 * i + 256], wb_ref[i])
        merged = term if merged is None else merged + term
    y_ref[...] = x + g1_ref[0] * _dot(merged.astype(_BF16), wo_ref[...])


def _merge(xa, mod_l, g_norm, o_cat, wg, bg, wb, wo, blocks_per_batch, n_batch):
    nt, d = xa.shape
    tb = TOK_BLOCK

    def mrow(i):
        return jnp.minimum(i // blocks_per_batch, n_batch)

    full = lambda a: pl.BlockSpec(a.shape, lambda i: (0,) * a.ndim)
    return pl.pallas_call(
        _merge_kernel,
        grid=(nt // tb,),
        in_specs=[
            pl.BlockSpec((tb, d), lambda i: (i, 0)),
            pl.BlockSpec((1, 1, d), lambda i: (mrow(i), 0, 0)),
            pl.BlockSpec((1, 1, d), lambda i: (mrow(i), 0, 1)),
            pl.BlockSpec((1, 1, d), lambda i: (mrow(i), 0, 2)),
            full(g_norm),
            pl.BlockSpec((tb, d), lambda i: (i, 0)),
            full(wg), full(bg), full(wb), full(wo),
        ],
        out_specs=pl.BlockSpec((tb, d), lambda i: (i, 0)),
        out_shape=jax.ShapeDtypeStruct((nt, d), _F32),
        compiler_params=_cparams(("arbitrary",)),
        name="branch_merge",
    )(xa, mod_l, mod_l, mod_l, g_norm, o_cat, wg, bg, wb, wo)


def _top_rows(s, payload, k):
    n = s.shape[0]
    iota = lax.broadcasted_iota(jnp.int32, s.shape, 0)
    vals, pays = [], []
    for _ in range(k):
        m = jnp.max(s, axis=0, keepdims=True)
        pos = jnp.min(jnp.where(s == m, iota, n), axis=0, keepdims=True)
        hit = iota == pos
        vals.append(m)
        pays.append(jnp.max(jnp.where(hit, payload, -1), axis=0, keepdims=True))
        s = jnp.where(hit, -jnp.inf, s)
    return jnp.concatenate(vals, axis=0), jnp.concatenate(pays, axis=0)


def _route_kernel(x_ref, sh_ref, sc_ref, g_ref, wqh_ref, wql_ref, skh_ref, skl_ref,
                  h_ref, e_ref, gate_ref, q_s, top_s, idx_s):
    h = _norm_mod(x_ref[...], g_ref[...], sh_ref[0], sc_ref[0])
    h_ref[...] = h.astype(_BF16)
    h_hi, h_lo = _split_bf16(h)
    q = _dot(h_hi, wqh_ref[...]) + _dot(h_hi, wql_ref[...]) + _dot(h_lo, wqh_ref[...])
    for hp in range(2 * PEER_HEADS):
        q_s[hp] = q[:, LANES * hp:LANES * (hp + 1)]

    key_iota = lax.broadcasted_iota(jnp.int32, (PEER_NKEYS, q.shape[0]), 0)

    def sub_scores(hp, c):
        q_hi, q_lo = _split_bf16(q_s[hp])
        sk_hi, sk_lo = skh_ref[hp], skl_ref[hp]
        s = (lax.dot_general(sk_hi, q_hi, _NT, preferred_element_type=_F32)
             + lax.dot_general(sk_hi, q_lo, _NT, preferred_element_type=_F32)
             + lax.dot_general(sk_lo, q_hi, _NT, preferred_element_type=_F32))
        vals, idx = _top_rows(s, key_iota, PEER_TOPK)
        top_s[hp] = vals
        idx_s[hp] = idx
        return c

    lax.fori_loop(0, 2 * PEER_HEADS, sub_scores, 0)

    def head(hd, c):
        s1, s2 = top_s[2 * hd], top_s[2 * hd + 1]
        i1, i2 = idx_s[2 * hd], idx_s[2 * hd + 1]
        cand = jnp.concatenate([s1[a:a + 1, :] + s2 for a in range(PEER_TOPK)], axis=0)
        cidx = jnp.concatenate([i1[a:a + 1, :] * PEER_NKEYS + i2 for a in range(PEER_TOPK)], axis=0)
        best, experts = _top_rows(cand, cidx, PEER_TOPK)
        ex = jnp.exp(best - best[0:1, :])
        gate_ref[hd] = ex / jnp.sum(ex, axis=0, keepdims=True)
        e_ref[hd] = experts
        return c

    lax.fori_loop(0, PEER_HEADS, head, 0)


def _peer_route(xa, mod_l, g_norm, wq_hi, wq_lo, sk_hi, sk_lo, blocks_per_batch, n_batch):
    nt, d = xa.shape
    tb = TOK_BLOCK

    def mrow(i):
        return jnp.minimum(i // blocks_per_batch, n_batch)

    full = lambda a: pl.BlockSpec(a.shape, lambda i: (0,) * a.ndim)
    return pl.pallas_call(
        _route_kernel,
        grid=(nt // tb,),
        in_specs=[
            pl.BlockSpec((tb, d), lambda i: (i, 0)),
            pl.BlockSpec((1, 1, d), lambda i: (mrow(i), 0, 3)),
            pl.BlockSpec((1, 1, d), lambda i: (mrow(i), 0, 4)),
            full(g_norm), full(wq_hi), full(wq_lo), full(sk_hi), full(sk_lo),
        ],
        out_specs=[
            pl.BlockSpec((tb, d), lambda i: (i, 0)),
            pl.BlockSpec((PEER_HEADS, PEER_TOPK, tb), lambda i: (0, 0, i)),
            pl.BlockSpec((PEER_HEADS, PEER_TOPK, tb), lambda i: (0, 0, i)),
        ],
        out_shape=[
            jax.ShapeDtypeStruct((nt, d), _BF16),
            jax.ShapeDtypeStruct((PEER_HEADS, PEER_TOPK, nt), jnp.int32),
            jax.ShapeDtypeStruct((PEER_HEADS, PEER_TOPK, nt), _F32),
        ],
        scratch_shapes=[
            pltpu.VMEM((2 * PEER_HEADS, tb, LANES), _F32),
            pltpu.VMEM((2 * PEER_HEADS, PEER_TOPK, tb), _F32),
            pltpu.VMEM((2 * PEER_HEADS, PEER_TOPK, tb), jnp.int32),
        ],
        compiler_params=_cparams(("arbitrary",)),
        name="peer_route",
    )(xa, mod_l, mod_l, g_norm, wq_hi, wq_lo, sk_hi, sk_lo)


def _gather_rows(idx_ref, tab_ref, tile_ref, n_pairs):
    def body(g, c):
        base = pl.multiple_of(g * (GATHER_UNROLL * PACK_ROWS), 8)
        for i in range(GATHER_UNROLL):
            row = pl.multiple_of(idx_ref[g * GATHER_UNROLL + i], PACK_ROWS)
            tile_ref[pl.ds(base + PACK_ROWS * i, PACK_ROWS), :] = tab_ref[pl.ds(row, PACK_ROWS), :]
        return c

    lax.fori_loop(0, n_pairs // GATHER_UNROLL, body, 0)


def _unpack_f32(w):
    lo = pltpu.bitcast(w << 16, _F32)
    hi = pltpu.bitcast(w & jnp.uint32(0xFFFF0000), _F32)
    return lo, hi


def _erf(x):
    return lax.erf(x)


def _peer_u_kernel(idx_ref, h_ref, gate_ref, tab_ref, w_ref, tile_ref):
    tb = h_ref.shape[0]
    n_pairs = tb * PEER_PAIRS
    _gather_rows(idx_ref, tab_ref, tile_ref, n_pairs)
    h = h_ref[...]
    half = D_MODEL // 2
    acc = jnp.zeros((n_pairs, tb), _F32)
    for j in range(PACK_ROWS):
        lo, hi = _unpack_f32(tile_ref[pl.ds(j, n_pairs, stride=PACK_ROWS), :])
        acc += lax.dot_general(lo.astype(_BF16), h[:, LANES * j:LANES * (j + 1)], _NT,
                               preferred_element_type=_F32)
        acc += lax.dot_general(hi.astype(_BF16), h[:, half + LANES * j:half + LANES * (j + 1)], _NT,
                               preferred_element_type=_F32)
    lane = lax.broadcasted_iota(jnp.int32, (PEER_PAIRS, tb), 1)
    a = jnp.zeros((PEER_PAIRS, tb), _F32)
    for t in range(tb):
        a += jnp.where(lane == t, acc[PEER_PAIRS * t:PEER_PAIRS * (t + 1), :], 0.0)
    act = 0.5 * a * (1.0 + _erf(a * (2.0 ** -0.5)))
    w_ref[0] = gate_ref[0] * act


def _peer_v_kernel(idx_ref, w_ref, x_ref, g2_ref, tab_ref, y_ref, tile_ref):
    tb = x_ref.shape[0]
    n_pairs = tb * PEER_PAIRS
    _gather_rows(idx_ref, tab_ref, tile_ref, n_pairs)
    w = w_ref[0]
    half = D_MODEL // 2
    g2 = g2_ref[0]
    for t in range(tb):
        wt = jnp.broadcast_to(w[:, t:t + 1], (PEER_PAIRS, LANES))
        for j in range(PACK_ROWS):
            rows = tile_ref[pl.ds(t * PEER_PAIRS * PACK_ROWS + j, PEER_PAIRS, stride=PACK_ROWS), :]
            lo, hi = _unpack_f32(rows)
            c0 = LANES * j
            y_ref[t:t + 1, c0:c0 + LANES] = (
                x_ref[t:t + 1, c0:c0 + LANES]
                + g2[:, c0:c0 + LANES] * jnp.sum(lo * wt, axis=0, keepdims=True))
            c1 = half + LANES * j
            y_ref[t:t + 1, c1:c1 + LANES] = (
                x_ref[t:t + 1, c1:c1 + LANES]
                + g2[:, c1:c1 + LANES] * jnp.sum(hi * wt, axis=0, keepdims=True))


def _table_spec(tab):
    return pl.BlockSpec(tab.shape, lambda i: (0, 0), pipeline_mode=pl.Buffered(1))


def _peer_u(idx, h, gate3, tab_u):
    nt, d = h.shape
    tb = PEER_TB
    n_pairs = tb * PEER_PAIRS
    return pl.pallas_call(
        _peer_u_kernel,
        grid=(nt // tb,),
        in_specs=[
            pl.BlockSpec((n_pairs,), lambda i: (i,), memory_space=pltpu.SMEM),
            pl.BlockSpec((tb, d), lambda i: (i, 0)),
            pl.BlockSpec((1, PEER_PAIRS, tb), lambda i: (i, 0, 0)),
            _table_spec(tab_u),
        ],
        out_specs=pl.BlockSpec((1, PEER_PAIRS, tb), lambda i: (i, 0, 0)),
        out_shape=jax.ShapeDtypeStruct((nt // tb, PEER_PAIRS, tb), _F32),
        scratch_shapes=[pltpu.VMEM((n_pairs * PACK_ROWS, LANES), jnp.uint32)],
        compiler_params=_cparams(("arbitrary",)),
        name="peer_expert_in",
    )(idx, h, gate3, tab_u)


def _peer_v(idx, w3, xa, mod_l, tab_v, blocks_per_batch, n_batch):
    nt, d = xa.shape
    tb = PEER_TB
    n_pairs = tb * PEER_PAIRS

    def mrow(i):
        return jnp.minimum(i // blocks_per_batch, n_batch)

    return pl.pallas_call(
        _peer_v_kernel,
        grid=(nt // tb,),
        in_specs=[
            pl.BlockSpec((n_pairs,), lambda i: (i,), memory_space=pltpu.SMEM),
            pl.BlockSpec((1, PEER_PAIRS, tb), lambda i: (i, 0, 0)),
            pl.BlockSpec((tb, d), lambda i: (i, 0)),
            pl.BlockSpec((1, 1, d), lambda i: (mrow(i), 0, 5)),
            _table_spec(tab_v),
        ],
        out_specs=pl.BlockSpec((tb, d), lambda i: (i, 0)),
        out_shape=jax.ShapeDtypeStruct((nt, d), _F32),
        scratch_shapes=[pltpu.VMEM((n_pairs * PACK_ROWS, LANES), jnp.uint32)],
        compiler_params=_cparams(("arbitrary",)),
        name="peer_expert_out",
    )(idx, w3, xa, mod_l, tab_v)


def _final_kernel(x_ref, g_ref, o_ref):
    x = x_ref[...]
    o_ref[...] = x * lax.rsqrt(jnp.mean(x * x, axis=-1, keepdims=True) + EPS) * g_ref[...]


def _final_norm(x, g):
    n, d = x.shape
    tb = TOK_BLOCK
    return pl.pallas_call(
        _final_kernel,
        grid=(n // tb,),
        in_specs=[pl.BlockSpec((tb, d), lambda i: (i, 0)), pl.BlockSpec((1, d), lambda i: (0, 0))],
        out_specs=pl.BlockSpec((tb, d), lambda i: (i, 0)),
        out_shape=jax.ShapeDtypeStruct((n, d), _F32),
        compiler_params=_cparams(("arbitrary",)),
        name="final_norm",
    )(x, g)


def _rope_tables(seq, dr):
    pos = jnp.arange(seq)
    rows = (pos // GRID_W).astype(_F32)
    cols = (pos % GRID_W).astype(_F32)
    h = dr // 2
    inv_freq = jnp.power(ROPE_BASE, -jnp.arange(0, h, 2, dtype=_F32) / h)
    ang_r = rows[:, None] * inv_freq[None, :]
    ang_c = cols[:, None] * inv_freq[None, :]
    cr, sr, cc, sc = jnp.cos(ang_r), jnp.sin(ang_r), jnp.cos(ang_c), jnp.sin(ang_c)
    z = jnp.zeros_like(sr)
    cos = jnp.concatenate([cr, cr, cc, cc], axis=-1)
    sa = jnp.concatenate([-sr, z, -sc, z], axis=-1)
    sb = jnp.concatenate([z, sr, z, sc], axis=-1)
    pad1 = jnp.ones((TOK_BLOCK, dr), _F32)
    pad0 = jnp.zeros((TOK_BLOCK, dr), _F32)
    return (jnp.concatenate([cos, pad1], axis=0), jnp.concatenate([sa, pad0], axis=0),
            jnp.concatenate([sb, pad0], axis=0))


def _pack_table(t):
    e, d = t.shape
    bits = lax.bitcast_convert_type(t.astype(_BF16), jnp.uint16).astype(jnp.uint32)
    packed = bits[:, :d // 2] | (bits[:, d // 2:] << 16)
    return packed.reshape(e * PACK_ROWS, LANES)


def _reorder_w_in(w_in):
    d = w_in.shape[0]
    main = jnp.concatenate([w_in[:, :384], w_in[:, 416:]], axis=1)
    kr = w_in[:, 384:416]
    return jnp.concatenate([main, kr, jnp.zeros((d, IN_COLS_PAD - COL_KR - MLA_ROPE), w_in.dtype)], axis=1)


def _heads(t, n_heads):
    b, l, _ = t.shape
    return t.reshape(b, l, n_heads, -1).transpose(0, 2, 1, 3)


def _unheads(t):
    b, h, l, d = t.shape
    return t.transpose(0, 2, 1, 3).reshape(b, l, h * d)


def kernel(x, c, ctx, c_ctx, w_mod, b_mod, norm_mix, norm_ffn, w_in, mla_q_norm, mla_kv_norm, mla_w_uq,
           mla_w_ukv, na_rpb, diff_lam_q1, diff_lam_k1, diff_lam_q2, diff_lam_k2, diff_subln, gqa_q_norm,
           gqa_k_norm, w_branch, w_gate, b_gate, w_out, peer_w_q, peer_subkeys, peer_u, peer_v, final_norm):
    bsz, seq, d = x.shape
    clen = ctx.shape[1]
    depth = w_mod.shape[0]
    n_lat, n_ctx = bsz * seq, bsz * clen
    assert d == D_MODEL and seq % TOK_BLOCK == 0 and clen % TOK_BLOCK == 0 and seq % GRID_W == 0
    assert bsz < MOD_ROWS
    blocks_per_batch = seq // TOK_BLOCK
    n_lat_blocks = n_lat // TOK_BLOCK
    tq = min(512, seq)

    cond = jnp.zeros((MOD_ROWS, d), _F32).at[:bsz].set(c).at[bsz].set(c_ctx)
    mod = _mod_all(cond, w_mod, b_mod)

    c32, sa32, sb32 = _rope_tables(seq, MLA_ROPE)
    c64, sa64, sb64 = _rope_tables(seq, GQA_HEAD_DIM)
    rows_t = seq + TOK_BLOCK
    one64 = jnp.ones((rows_t, MLA_NOPE), _F32)
    zero64 = jnp.zeros((rows_t, MLA_NOPE), _F32)
    cq = jnp.tile(jnp.concatenate([one64, c32], axis=1), (1, MLA_HEADS))
    saq = jnp.tile(jnp.concatenate([zero64, sa32], axis=1), (1, MLA_HEADS))
    sbq = jnp.tile(jnp.concatenate([zero64, sb32], axis=1), (1, MLA_HEADS))
    tabs = (cq, saq, sbq, jnp.tile(c32, (1, 8)), jnp.tile(sa32, (1, 8)), jnp.tile(sb32, (1, 8)),
            jnp.tile(c64, (1, 4)), jnp.tile(sa64, (1, 4)), jnp.tile(sb64, (1, 4)))
    seg = jnp.asarray(np.kron(np.eye(256 // GQA_HEAD_DIM), np.ones((GQA_HEAD_DIM, GQA_HEAD_DIM))), _BF16)

    n_rows = seq // GRID_W
    xa = jnp.concatenate([x.reshape(n_lat, d), ctx.reshape(n_ctx, d)], axis=0)

    for l in range(depth):
        mod_l = mod[l].reshape(MOD_ROWS, 1, N_MOD * d)
        g_mix = norm_mix[l].reshape(1, d)
        mq, mkv, mkr, na, df, gqa = _proj_prep(
            xa, mod_l, g_mix, _reorder_w_in(w_in[l]).astype(_BF16),
            mla_q_norm[l].reshape(1, -1), mla_kv_norm[l].reshape(1, -1),
            mla_w_uq[l].astype(_BF16), mla_w_ukv[l].astype(_BF16),
            jnp.tile(gqa_q_norm[l], GQA_Q_HEADS).reshape(1, -1), jnp.tile(gqa_k_norm[l], 4).reshape(1, -1),
            seg, tabs, n_lat_blocks, blocks_per_batch, bsz, blocks_per_batch)

        def both(t):
            return jnp.concatenate([t[:n_lat].reshape(bsz, seq, -1), t[n_lat:].reshape(bsz, clen, -1)], axis=1)

        q = _heads(both(mq), MLA_HEADS)
        kv = _heads(both(mkv), MLA_HEADS)
        kr = jnp.broadcast_to(both(mkr)[:, None, :, :MLA_ROPE], kv.shape[:3] + (MLA_ROPE,))
        k = jnp.concatenate([kv[..., :MLA_NOPE], kr], axis=-1)
        v = kv[..., MLA_NOPE:]
        o_a = _attention(q[:, :, :seq], k, v, MLA_QK ** -0.5, tq)
        o_a_c = _attention(q[:, :, seq:], k[:, :, seq:], v[:, :, seq:], MLA_QK ** -0.5, clen)

        nab = both(na)
        q, k, v = (_heads(nab[..., 256 * i:256 * i + 256], NA_HEADS) for i in range(3))
        bias = _na_bias_table(na_rpb[l], n_rows)
        o_b = _na_attention(q[:, :, :seq], k[:, :, :seq], v[:, :, :seq], k[:, :, seq:], v[:, :, seq:], bias)
        o_b_c = _attention(q[:, :, seq:], k[:, :, seq:], v[:, :, seq:], NA_HEAD_DIM ** -0.5, clen)

        dfb = both(df)
        q, k, v = (_heads(dfb[..., 256 * i:256 * i + 256], DIFF_HEADS) for i in range(3))
        lam_vecs = jnp.stack([diff_lam_q1[l], diff_lam_k1[l], diff_lam_q2[l], diff_lam_k2[l]], axis=0)
        lam_init = 0.8 - 0.6 * math.exp(-0.3 * l)
        sub = diff_subln[l].reshape(1, -1)
        o_c = _diff_attention(q[:, :, :seq], k, v, lam_vecs, sub, lam_init, tq)
        o_c_c = _diff_attention(q[:, :, seq:], k[:, :, seq:], v[:, :, seq:], lam_vecs, sub, lam_init, clen)

        gb = both(gqa)
        q = _heads(gb[..., :256], GQA_Q_HEADS)
        k = _heads(gb[..., 256:384], GQA_KV_HEADS)
        v = _heads(gb[..., 384:512], GQA_KV_HEADS)
        o_d = _attention(q[:, :, :seq], k, v, GQA_HEAD_DIM ** -0.5, tq)
        o_d_c = _attention(q[:, :, seq:], k[:, :, seq:], v[:, :, seq:], GQA_HEAD_DIM ** -0.5, clen)

        o_lat = jnp.concatenate([_unheads(o) for o in (o_a, o_b, o_c, o_d)], axis=-1).reshape(n_lat, d)
        o_ctx = jnp.concatenate([_unheads(o) for o in (o_a_c, o_b_c, o_c_c, o_d_c)], axis=-1).reshape(n_ctx, d)
        o_cat = jnp.concatenate([o_lat, o_ctx], axis=0)

        xa = _merge(xa, mod_l, g_mix, o_cat, w_gate[l].astype(_BF16), b_gate[l].reshape(4, 1, d),
                    w_branch[l].astype(_BF16), w_out[l].astype(_BF16), blocks_per_batch, bsz)

        wq_hi, wq_lo = _split_bf16(peer_w_q[l])
        sk = peer_subkeys[l].reshape(2 * PEER_HEADS, PEER_NKEYS, PEER_QDIM // 2)
        sk_hi, sk_lo = _split_bf16(sk)
        h2, experts, gates = _peer_route(xa, mod_l, norm_ffn[l].reshape(1, d), wq_hi, wq_lo, sk_hi, sk_lo,
                                         blocks_per_batch, bsz)
        nt = n_lat + n_ctx
        idx = (experts.reshape(PEER_PAIRS, nt).T * PACK_ROWS).reshape(-1)
        gate3 = gates.reshape(PEER_PAIRS, nt // PEER_TB, PEER_TB).transpose(1, 0, 2)
        w3 = _peer_u(idx, h2, gate3, _pack_table(peer_u[l]))
        xa = _peer_v(idx, w3, xa, mod_l, _pack_table(peer_v[l]), seq // PEER_TB, bsz)

    return _final_norm(xa[:n_lat], final_norm.reshape(1, d)).reshape(bsz, seq, d)
```

```python
import functools
import math

import numpy as np
import jax
import jax.numpy as jnp
from jax import lax
from jax.experimental import pallas as pl
from jax.experimental.pallas import tpu as pltpu

D_MODEL = 1024
GRID_W = 64
EPS = 1e-6
ROPE_BASE = 10000.0
N_MOD = 6

MLA_HEADS = 4
MLA_Q_RANK = 256
MLA_KV_RANK = 128
MLA_NOPE = 64
MLA_ROPE = 32
MLA_V = 64
MLA_QK = MLA_NOPE + MLA_ROPE

NA_HEADS = 4
NA_HEAD_DIM = 64
NA_WIN_ROWS = 8
NA_WIN_COLS = 16

DIFF_HEADS = 4
DIFF_HALF = 32
DIFF_V = 2 * DIFF_HALF

GQA_Q_HEADS = 4
GQA_KV_HEADS = 2
GQA_HEAD_DIM = 64

PEER_HEADS = 8
PEER_NKEYS = 128
PEER_EXPERTS = PEER_NKEYS * PEER_NKEYS
PEER_QDIM = 256
PEER_TOPK = 16
PEER_PAIRS = PEER_HEADS * PEER_TOPK

LANES = 128
SUBLANES = 8
VMEM_LIMIT_BYTES = 56 * 1024 * 1024

COL_QC = 0
COL_KVC = 256
COL_NA = 384
COL_DIFF = 1152
COL_GQA = 1920
COL_KR = 2432
IN_COLS_PAD = 2560
MOD_ROWS = 16

TOK_BLOCK = 256
ROUTE_BLOCK = 512
KEY_CHUNK = 512
HEAD_V = 64
PEER_TB = 32
PACK_ROWS = 4
TOK_ROWS = PEER_PAIRS * PACK_ROWS
U_GROUP = 4

_NT = (((1,), (1,)), ((), ()))
_F32 = jnp.float32
_BF16 = jnp.bfloat16


def _cparams(sem):
    return pltpu.CompilerParams(dimension_semantics=sem, vmem_limit_bytes=VMEM_LIMIT_BYTES)


def _split_bf16(a):
    hi = a.astype(_BF16)
    lo = (a - hi.astype(_F32)).astype(_BF16)
    return hi, lo


def _dot(a, b):
    return jnp.dot(a, b, preferred_element_type=_F32)


def _norm_mod(x, g, shift, scale):
    y = x * lax.rsqrt(jnp.mean(x * x, axis=-1, keepdims=True) + EPS)
    return (y * g) * (1.0 + scale) + shift


def _mod_kernel(c_ref, w_ref, b_ref, o_ref):
    c = c_ref[...]
    s = c * (1.0 / (1.0 + jnp.exp(-c)))
    s_hi, s_lo = _split_bf16(s)
    w_hi, w_lo = _split_bf16(w_ref[0])
    o_ref[0] = _dot(s_hi, w_hi) + _dot(s_hi, w_lo) + _dot(s_lo, w_hi) + b_ref[0]


def _mod_all(cond, w_mod, b_mod):
    depth, d, n = w_mod.shape
    tn = 1536
    return pl.pallas_call(
        _mod_kernel,
        grid=(depth, n // tn),
        in_specs=[
            pl.BlockSpec((MOD_ROWS, d), lambda l, j: (0, 0)),
            pl.BlockSpec((1, d, tn), lambda l, j: (l, 0, j)),
            pl.BlockSpec((1, 1, tn), lambda l, j: (l, 0, j)),
        ],
        out_specs=pl.BlockSpec((1, MOD_ROWS, tn), lambda l, j: (l, 0, j)),
        out_shape=jax.ShapeDtypeStruct((depth, MOD_ROWS, n), _F32),
        compiler_params=_cparams(("arbitrary", "arbitrary")),
        name="adaln_mod",
    )(cond, w_mod, b_mod.reshape(depth, 1, n))


def _rope(t, cos, sa, sb, dist):
    w = t.shape[-1]
    return t * cos + pltpu.roll(t, w - dist, 1) * sa + pltpu.roll(t, dist, 1) * sb


def _seg_mean_sq(t, seg_ref, seg_len):
    hi, lo = _split_bf16(t * t)
    seg = seg_ref[...]
    return (_dot(hi, seg) + _dot(lo, seg)) * (1.0 / seg_len)


def _proj_kernel(x_ref, sh_ref, sc_ref, g_ref, win_ref, qn_ref, kvn_ref, wuq_ref, wukv_ref,
                 gq_ref, gk_ref, seg_ref, cq_ref, saq_ref, sbq_ref, c32_ref, sa32_ref, sb32_ref,
                 c64_ref, sa64_ref, sb64_ref,
                 mq_ref, mkv_ref, mkr_ref, na_ref, df_ref, gqa_ref):
    h = _norm_mod(x_ref[...], g_ref[...], sh_ref[0], sc_ref[0])
    p = _dot(h.astype(_BF16), win_ref[...])

    qc = p[:, COL_QC:COL_QC + MLA_Q_RANK]
    qn = qc * lax.rsqrt(jnp.mean(qc * qc, axis=-1, keepdims=True) + EPS) * qn_ref[...]
    q = _dot(qn.astype(_BF16), wuq_ref[...])
    mq_ref[...] = (_rope(q, cq_ref[...], saq_ref[...], sbq_ref[...], MLA_ROPE // 4) * MLA_QK ** -0.5).astype(_BF16)
    kvc = p[:, COL_KVC:COL_KVC + MLA_KV_RANK]
    kvn = kvc * lax.rsqrt(jnp.mean(kvc * kvc, axis=-1, keepdims=True) + EPS) * kvn_ref[...]
    mkv_ref[...] = _dot(kvn.astype(_BF16), wukv_ref[...]).astype(_BF16)
    kr = p[:, COL_KR:COL_KR + LANES]
    mkr_ref[...] = _rope(kr, c32_ref[:, :LANES], sa32_ref[:, :LANES], sb32_ref[:, :LANES],
                         MLA_ROPE // 4).astype(_BF16)

    na_ref[:, 0:256] = (p[:, COL_NA:COL_NA + 256] * NA_HEAD_DIM ** -0.5).astype(_BF16)
    na_ref[:, 256:768] = p[:, COL_NA + 256:COL_NA + 768].astype(_BF16)

    dq = p[:, COL_DIFF:COL_DIFF + 256]
    dk = p[:, COL_DIFF + 256:COL_DIFF + 512]
    c32, sa32, sb32 = c32_ref[...], sa32_ref[...], sb32_ref[...]
    df_ref[:, 0:256] = (_rope(dq, c32, sa32, sb32, DIFF_HALF // 4) * DIFF_HALF ** -0.5).astype(_BF16)
    df_ref[:, 256:512] = _rope(dk, c32, sa32, sb32, DIFF_HALF // 4).astype(_BF16)
    df_ref[:, 512:768] = p[:, COL_DIFF + 512:COL_DIFF + 768].astype(_BF16)

    gq = p[:, COL_GQA:COL_GQA + 256]
    gk = p[:, COL_GQA + 256:COL_GQA + 384]
    gqn = gq * lax.rsqrt(_seg_mean_sq(gq, seg_ref, GQA_HEAD_DIM) + EPS) * gq_ref[...]
    gk2 = jnp.concatenate([gk, gk], axis=-1)
    gkn = gk2 * lax.rsqrt(_seg_mean_sq(gk2, seg_ref, GQA_HEAD_DIM) + EPS) * gk_ref[...]
    c64, sa64, sb64 = c64_ref[...], sa64_ref[...], sb64_ref[...]
    gqa_ref[:, 0:256] = (_rope(gqn, c64, sa64, sb64, GQA_HEAD_DIM // 4) * GQA_HEAD_DIM ** -0.5).astype(_BF16)
    gqa_ref[:, 256:384] = _rope(gkn, c64, sa64, sb64, GQA_HEAD_DIM // 4)[:, :128].astype(_BF16)
    gqa_ref[:, 384:512] = p[:, COL_GQA + 384:COL_GQA + 512].astype(_BF16)


def _proj_prep(xa, mod_l, g_norm, win, qn, kvn, wuq, wukv, gq, gk, seg, tabs, n_lat_blocks, blocks_per_batch,
               n_batch, pos_blocks):
    nt, d = xa.shape
    tb = TOK_BLOCK

    def mrow(i):
        return jnp.minimum(i // blocks_per_batch, n_batch)

    def trow(i):
        return jnp.where(i < n_lat_blocks, i % pos_blocks, pos_blocks)

    full = lambda a: pl.BlockSpec(a.shape, lambda i: (0,) * a.ndim)
    tab_specs = [pl.BlockSpec((tb, t.shape[1]), lambda i: (trow(i), 0)) for t in tabs]
    widths = (384, 512, 128, 768, 768, 512)
    return pl.pallas_call(
        _proj_kernel,
        grid=(nt // tb,),
        in_specs=[
            pl.BlockSpec((tb, d), lambda i: (i, 0)),
            pl.BlockSpec((1, 1, d), lambda i: (mrow(i), 0, 0)),
            pl.BlockSpec((1, 1, d), lambda i: (mrow(i), 0, 1)),
            full(g_norm), full(win), full(qn), full(kvn), full(wuq), full(wukv), full(gq), full(gk), full(seg),
        ] + tab_specs,
        out_specs=[pl.BlockSpec((tb, w), lambda i: (i, 0)) for w in widths],
        out_shape=[jax.ShapeDtypeStruct((nt, w), _BF16) for w in widths],
        compiler_params=_cparams(("arbitrary",)),
        name="mix_in_proj",
    )(xa, mod_l, mod_l, g_norm, win, qn, kvn, wuq, wukv, gq, gk, seg, *tabs)


def _online_step(q, k, v, m_ref, acc_ref):
    s = lax.dot_general(q, k, _NT, preferred_element_type=_F32)
    m_old = m_ref[...]
    m_new = jnp.maximum(m_old, jnp.max(s, axis=-1, keepdims=True))
    e = jnp.exp((s - m_new).astype(_BF16))
    acc_ref[...] = jnp.exp(m_old - m_new) * acc_ref[...] + _dot(e, v)
    m_ref[...] = m_new


def _sweep_keys(qs, k_ref, v_ref, kc_ref, vc_ref, m_ref, acc_ref, n_lat_chunks):
    m_ref[...] = jnp.full(m_ref.shape, -jnp.inf, _F32)
    acc_ref[...] = jnp.zeros(acc_ref.shape, _F32)

    def body(c, carry):
        start = pl.multiple_of(c * KEY_CHUNK, KEY_CHUNK)
        k = k_ref[0, 0, pl.ds(start, KEY_CHUNK), :]
        v = v_ref[0, 0, pl.ds(start, KEY_CHUNK), :]
        for i, q in enumerate(qs):
            _online_step(q, k, v, m_ref.at[i], acc_ref.at[i])
        return carry

    if n_lat_chunks:
        lax.fori_loop(0, n_lat_chunks, body, 0)
    for i, q in enumerate(qs):
        _online_step(q, kc_ref[0, 0], vc_ref[0, 0], m_ref.at[i], acc_ref.at[i])


def _normalised(acc, dv):
    return acc[:, :dv] / acc[:, dv:dv + 1]


def _attn_kernel(q_ref, k_ref, v_ref, kc_ref, vc_ref, o_ref, m_ref, acc_ref, *, n_lat_chunks, dv):
    _sweep_keys([q_ref[0, 0]], k_ref, v_ref, kc_ref, vc_ref, m_ref, acc_ref, n_lat_chunks)
    o_ref[0, 0] = _normalised(acc_ref[0], dv).astype(o_ref.dtype)


def _diff_kernel(q_ref, k_ref, v_ref, kc_ref, vc_ref, lam_ref, sub_ref, o_ref, m_ref, acc_ref, *,
                 n_lat_chunks, dv, lam_init):
    lam = (jnp.exp(jnp.sum(lam_ref[0:1, :] * lam_ref[1:2, :], axis=-1, keepdims=True))
           - jnp.exp(jnp.sum(lam_ref[2:3, :] * lam_ref[3:4, :], axis=-1, keepdims=True)) + lam_init)
    q = q_ref[0, 0]
    first = lax.broadcasted_iota(jnp.int32, q.shape, 1) < DIFF_HALF
    zero = jnp.zeros_like(q)
    qs = [jnp.where(first, q, zero), jnp.where(first, zero, q)]
    _sweep_keys(qs, k_ref, v_ref, kc_ref, vc_ref, m_ref, acc_ref, n_lat_chunks)
    o = _normalised(acc_ref[0], dv) - lam * _normalised(acc_ref[1], dv)
    y = o * lax.rsqrt(jnp.mean(o * o, axis=-1, keepdims=True) + EPS) * sub_ref[...]
    o_ref[0, 0] = (y * (1.0 - lam_init)).astype(o_ref.dtype)


def _attention(q, k, v, kc, vc, tq, use_latent, diff=None):
    b, hq, lq, dq = q.shape
    hk, l = k.shape[1], k.shape[2]
    c = kc.shape[2]
    group = hq // hk
    n_maps = 2 if diff else 1
    n_lat_chunks = l // KEY_CHUNK if use_latent else 0
    dv = HEAD_V
    kv_map = lambda bi, h, i: (bi, h // group, 0, 0)
    in_specs = [
        pl.BlockSpec((1, 1, tq, dq), lambda bi, h, i: (bi, h, i, 0)),
        pl.BlockSpec((1, 1, l, dq), kv_map),
        pl.BlockSpec((1, 1, l, LANES), kv_map),
        pl.BlockSpec((1, 1, c, dq), kv_map),
        pl.BlockSpec((1, 1, c, LANES), kv_map),
    ]
    args = [q, k, v, kc, vc]
    if diff:
        lam_vecs, subln, lam_init = diff
        body = functools.partial(_diff_kernel, n_lat_chunks=n_lat_chunks, dv=dv, lam_init=lam_init)
        in_specs += [pl.BlockSpec(lam_vecs.shape, lambda bi, h, i: (0, 0)),
                     pl.BlockSpec(subln.shape, lambda bi, h, i: (0, 0))]
        args += [lam_vecs, subln]
    else:
        body = functools.partial(_attn_kernel, n_lat_chunks=n_lat_chunks, dv=dv)
    return pl.pallas_call(
        body,
        grid=(b, hq, lq // tq),
        in_specs=in_specs,
        out_specs=pl.BlockSpec((1, 1, tq, dv), lambda bi, h, i: (bi, h, i, 0)),
        out_shape=jax.ShapeDtypeStruct((b, hq, lq, dv), _BF16),
        scratch_shapes=[pltpu.VMEM((n_maps, tq, 1), _F32), pltpu.VMEM((n_maps, tq, LANES), _F32)],
        compiler_params=_cparams(("arbitrary", "arbitrary", "arbitrary")),
        name="diff_attention" if diff else "softmax_attention",
    )(*args)


def _na_kernel(q_ref, k_ref, v_ref, kc_ref, vc_ref, bias_ref, o_ref, *, n_rows, kr):
    r = pl.program_id(2)
    rs = jnp.clip(r - kr // 2, 0, n_rows - kr)
    start = pl.multiple_of(rs * GRID_W, GRID_W)
    q = q_ref[0, 0]
    k_win = k_ref[0, 0, pl.ds(start, kr * GRID_W), :]
    v_win = v_ref[0, 0, pl.ds(start, kr * GRID_W), :]
    s_loc = lax.dot_general(q, k_win, _NT, preferred_element_type=_F32) + bias_ref[0, r - rs]
    s_ctx = lax.dot_general(q, kc_ref[0, 0], _NT, preferred_element_type=_F32)
    m = jnp.maximum(jnp.max(s_loc, axis=-1, keepdims=True), jnp.max(s_ctx, axis=-1, keepdims=True))
    e_loc = jnp.exp((s_loc - m).astype(_BF16))
    e_ctx = jnp.exp((s_ctx - m).astype(_BF16))
    acc = _dot(e_loc, v_win) + _dot(e_ctx, vc_ref[0, 0])
    o_ref[0, 0] = _normalised(acc, NA_HEAD_DIM).astype(o_ref.dtype)


def _na_attention(q, k, v, kc, vc, bias):
    b, h, l, d = q.shape
    c = kc.shape[2]
    n_rows = l // GRID_W
    kr = min(NA_WIN_ROWS, n_rows)
    return pl.pallas_call(
        functools.partial(_na_kernel, n_rows=n_rows, kr=kr),
        grid=(b, h, n_rows),
        in_specs=[
            pl.BlockSpec((1, 1, GRID_W, d), lambda bi, hh, r: (bi, hh, r, 0)),
            pl.BlockSpec((1, 1, l, d), lambda bi, hh, r: (bi, hh, 0, 0)),
            pl.BlockSpec((1, 1, l, LANES), lambda bi, hh, r: (bi, hh, 0, 0)),
            pl.BlockSpec((1, 1, c, d), lambda bi, hh, r: (bi, hh, 0, 0)),
            pl.BlockSpec((1, 1, c, LANES), lambda bi, hh, r: (bi, hh, 0, 0)),
            pl.BlockSpec((1,) + bias.shape[1:], lambda bi, hh, r: (hh, 0, 0, 0)),
        ],
        out_specs=pl.BlockSpec((1, 1, GRID_W, d), lambda bi, hh, r: (bi, hh, r, 0)),
        out_shape=jax.ShapeDtypeStruct((b, h, l, d), _BF16),
        compiler_params=_cparams(("arbitrary", "arbitrary", "arbitrary")),
        name="neighbourhood_attention",
    )(q, k, v, kc, vc, bias)


def _na_bias_table(rpb, n_rows):
    kr = min(NA_WIN_ROWS, n_rows)
    nc = NA_WIN_COLS
    col_q = np.arange(GRID_W)
    col_start = np.clip(col_q - nc // 2, 0, GRID_W - nc)
    key_c = np.arange(GRID_W)
    inside = (key_c[None, :] >= col_start[:, None]) & (key_c[None, :] < col_start[:, None] + nc)
    col_off = key_c[None, :] - col_q[:, None] + (NA_WIN_COLS - 1)
    col_sel = (inside[:, :, None] & (col_off[:, :, None] == np.arange(2 * NA_WIN_COLS - 1))).astype(np.float32)
    row_off = np.arange(kr)[None, :] - np.arange(kr)[:, None] + (NA_WIN_ROWS - 1)
    row_sel = (row_off[:, :, None] == np.arange(2 * NA_WIN_ROWS - 1)).astype(np.float32)
    tab = jnp.einsum('hrj,dir,cxj->hdcix', rpb, row_sel, col_sel, precision=lax.Precision.HIGHEST)
    tab = jnp.where(inside[None, None, :, None, :], tab, -1e30)
    return tab.reshape(rpb.shape[0], kr, GRID_W, kr * GRID_W).astype(_F32)


def _merge_kernel(x_ref, sh_ref, sc_ref, g1_ref, g_ref, o_ref, wg_ref, bg_ref, wb_ref, wo_ref, y_ref):
    x = x_ref[...]
    hb = _norm_mod(x, g_ref[...], sh_ref[0], sc_ref[0]).astype(_BF16)
    o = o_ref[...]
    merged = None
    for i in range(4):
        gate = 1.0 / (1.0 + jnp.exp(-(_dot(hb, wg_ref[i]) + bg_ref[i])))
        term = gate * _dot(o[:, 256 * i:256 * i + 256], wb_ref[i])
        merged = term if merged is None else merged + term
    y_ref[...] = x + g1_ref[0] * _dot(merged.astype(_BF16), wo_ref[...])


def _merge(xa, mod_l, g_norm, o_cat, wg, bg, wb, wo, blocks_per_batch, n_batch):
    nt, d = xa.shape
    tb = TOK_BLOCK

    def mrow(i):
        return jnp.minimum(i // blocks_per_batch, n_batch)

    full = lambda a: pl.BlockSpec(a.shape, lambda i: (0,) * a.ndim)
    return pl.pallas_call(
        _merge_kernel,
        grid=(nt // tb,),
        in_specs=[
            pl.BlockSpec((tb, d), lambda i: (i, 0)),
            pl.BlockSpec((1, 1, d), lambda i: (mrow(i), 0, 0)),
            pl.BlockSpec((1, 1, d), lambda i: (mrow(i), 0, 1)),
            pl.BlockSpec((1, 1, d), lambda i: (mrow(i), 0, 2)),
            full(g_norm),
            pl.BlockSpec((tb, d), lambda i: (i, 0)),
            full(wg), full(bg), full(wb), full(wo),
        ],
        out_specs=pl.BlockSpec((tb, d), lambda i: (i, 0)),
        out_shape=jax.ShapeDtypeStruct((nt, d), _F32),
        compiler_params=_cparams(("arbitrary",)),
        name="branch_merge",
    )(xa, mod_l, mod_l, mod_l, g_norm, o_cat, wg, bg, wb, wo)


def _top_rows(s, payload, k):
    n = s.shape[0]
    iota = lax.broadcasted_iota(jnp.int32, s.shape, 0).astype(_F32)
    vals, pays = [], []
    for _ in range(k):
        m = jnp.max(s, axis=0, keepdims=True)
        pos = jnp.min(jnp.where(s == m, iota, float(n)), axis=0, keepdims=True)
        hit = iota == pos
        vals.append(m)
        if payload is None:
            pays.append(pos)
        else:
            pays.append(jnp.max(jnp.where(hit, payload, -1.0), axis=0, keepdims=True))
        s = jnp.where(hit, -jnp.inf, s)
    return jnp.concatenate(vals, axis=0), jnp.concatenate(pays, axis=0)


def _candidate_grid(s1, s2, combine):
    k = PEER_TOPK
    blocks = [combine(s1[0:1, :], s2)]
    blocks += [combine(s1[a:a + 1, :], s2[0:k // 2, :]) for a in range(1, k // 2)]
    blocks.append(combine(s1[k // 2:k, :], s2[0:1, :]))
    return jnp.concatenate(blocks, axis=0)


def _route_kernel(x_ref, sh_ref, sc_ref, g_ref, wqh_ref, wql_ref, skh_ref, skl_ref,
                  h_ref, e_ref, gate_ref, q_s, top_s, idx_s):
    h = _norm_mod(x_ref[...], g_ref[...], sh_ref[0], sc_ref[0])
    h_ref[...] = h.astype(_BF16)
    h_hi, h_lo = _split_bf16(h)
    q = _dot(h_hi, wqh_ref[...]) + _dot(h_hi, wql_ref[...]) + _dot(h_lo, wqh_ref[...])
    for hp in range(2 * PEER_HEADS):
        q_s[hp] = q[:, LANES * hp:LANES * (hp + 1)]

    def sub_scores(hp, c):
        q_hi, q_lo = _split_bf16(q_s[hp])
        sk_hi, sk_lo = skh_ref[hp], skl_ref[hp]
        s = (lax.dot_general(sk_hi, q_hi, _NT, preferred_element_type=_F32)
             + lax.dot_general(sk_hi, q_lo, _NT, preferred_element_type=_F32)
             + lax.dot_general(sk_lo, q_hi, _NT, preferred_element_type=_F32))
        vals, idx = _top_rows(s, None, PEER_TOPK)
        top_s[hp] = vals
        idx_s[hp] = idx
        return c

    lax.fori_loop(0, 2 * PEER_HEADS, sub_scores, 0)

    def head(hd, c):
        cand = _candidate_grid(top_s[2 * hd], top_s[2 * hd + 1], lambda a, b: a + b)
        cidx = _candidate_grid(idx_s[2 * hd], idx_s[2 * hd + 1], lambda a, b: a * float(PEER_NKEYS) + b)
        best, experts = _top_rows(cand, cidx, PEER_TOPK)
        ex = jnp.exp(best - best[0:1, :])
        gate_ref[hd] = ex / jnp.sum(ex, axis=0, keepdims=True)
        e_ref[hd] = experts.astype(jnp.int32)
        return c

    lax.fori_loop(0, PEER_HEADS, head, 0)


def _peer_route(xa, mod_l, g_norm, wq_hi, wq_lo, sk_hi, sk_lo, tb, seq, n_batch):
    nt, d = xa.shape

    def mrow(i):
        return jnp.minimum(i * tb // seq, n_batch)

    full = lambda a: pl.BlockSpec(a.shape, lambda i: (0,) * a.ndim)
    return pl.pallas_call(
        _route_kernel,
        grid=(nt // tb,),
        in_specs=[
            pl.BlockSpec((tb, d), lambda i: (i, 0)),
            pl.BlockSpec((1, 1, d), lambda i: (mrow(i), 0, 3)),
            pl.BlockSpec((1, 1, d), lambda i: (mrow(i), 0, 4)),
            full(g_norm), full(wq_hi), full(wq_lo), full(sk_hi), full(sk_lo),
        ],
        out_specs=[
            pl.BlockSpec((tb, d), lambda i: (i, 0)),
            pl.BlockSpec((PEER_HEADS, PEER_TOPK, tb), lambda i: (0, 0, i)),
            pl.BlockSpec((PEER_HEADS, PEER_TOPK, tb), lambda i: (0, 0, i)),
        ],
        out_shape=[
            jax.ShapeDtypeStruct((nt, d), _BF16),
            jax.ShapeDtypeStruct((PEER_HEADS, PEER_TOPK, nt), jnp.int32),
            jax.ShapeDtypeStruct((PEER_HEADS, PEER_TOPK, nt), _F32),
        ],
        scratch_shapes=[
            pltpu.VMEM((2 * PEER_HEADS, tb, LANES), _F32),
            pltpu.VMEM((2 * PEER_HEADS, PEER_TOPK, tb), _F32),
            pltpu.VMEM((2 * PEER_HEADS, PEER_TOPK, tb), _F32),
        ],
        compiler_params=_cparams(("arbitrary",)),
        name="peer_route",
    )(xa, mod_l, mod_l, g_norm, wq_hi, wq_lo, sk_hi, sk_lo)


def _gather_token(idx_ref, tab_ref, tile_ref, t):
    dst = t * TOK_ROWS
    if not isinstance(t, int):
        dst = pl.multiple_of(dst, TOK_ROWS)
    for p in range(PEER_PAIRS):
        row = pl.multiple_of(idx_ref[t * PEER_PAIRS + p], PACK_ROWS)
        tile_ref[pl.ds(dst + PACK_ROWS * p, PACK_ROWS), :] = tab_ref[pl.ds(row, PACK_ROWS), :]


def _first_block_gather(idx_ref, tab_ref, tile_ref, tb):
    @pl.when(pl.program_id(0) == 0)
    def _():
        def body(t, c):
            _gather_token(idx_ref, tab_ref, tile_ref, t)
            return c

        lax.fori_loop(0, tb, body, 0)


def _ping_pong(fn, tile_a, tile_b):
    parity = pl.program_id(0) % 2

    @pl.when(parity == 0)
    def _():
        fn(tile_a, tile_b)

    @pl.when(parity == 1)
    def _():
        fn(tile_b, tile_a)


def _unpack_f32(w):
    lo = pltpu.bitcast(w << 16, _F32)
    hi = pltpu.bitcast(w & jnp.uint32(0xFFFF0000), _F32)
    return lo, hi


def _peer_u_kernel(idx_ref, idx_next_ref, h_ref, gate_ref, tab_ref, w_ref, tile_a, tile_b):
    tb = h_ref.shape[0]
    _first_block_gather(idx_ref, tab_ref, tile_a, tb)
    lane = lax.broadcasted_iota(jnp.int32, (PEER_PAIRS, tb), 1)

    def run(src, dst):
        h = h_ref[...]
        a = jnp.zeros((PEER_PAIRS, tb), _F32)
        for g in range(tb // U_GROUP):
            t0 = g * U_GROUP
            los, his = [], []
            for j in range(PACK_ROWS):
                lo, hi = _unpack_f32(src[pl.ds(t0 * TOK_ROWS + j, U_GROUP * PEER_PAIRS, stride=PACK_ROWS), :])
                los.append(lo.astype(_BF16))
                his.append(hi.astype(_BF16))
            rows = jnp.concatenate(los + his, axis=1)
            acc = lax.dot_general(rows, h, _NT, preferred_element_type=_F32)
            for k in range(U_GROUP):
                a = a + jnp.where(lane == t0 + k, acc[PEER_PAIRS * k:PEER_PAIRS * (k + 1), :], 0.0)
            for k in range(U_GROUP):
                _gather_token(idx_next_ref, tab_ref, dst, t0 + k)
        act = 0.5 * a * (1.0 + lax.erf(a * (2.0 ** -0.5)))
        w_ref[0] = gate_ref[0] * act

    _ping_pong(run, tile_a, tile_b)


def _peer_v_kernel(idx_ref, idx_next_ref, w_ref, x_ref, g2_ref, pe_ref, po_ref, tab_ref, y_ref,
                   tile_a, tile_b, lhs_ref):
    tb = x_ref.shape[0]
    _first_block_gather(idx_ref, tab_ref, tile_a, tb)

    @pl.when(pl.program_id(0) == 0)
    def _():
        lhs_ref[...] = jnp.zeros(lhs_ref.shape, lhs_ref.dtype)

    w_hi, w_lo = _split_bf16(w_ref[0].T)
    v = 0
    for spread_ref in (pe_ref, po_ref):
        for part in (w_hi, w_lo):
            val = _dot(part, spread_ref[...])
            for hf in range(2):
                lhs_ref[hf, pl.ds(v, tb, stride=SUBLANES), :] = val[:, LANES * hf:LANES * (hf + 1)]
            v += 1
    sub = lax.broadcasted_iota(jnp.int32, (SUBLANES, LANES), 0)

    def run(src, dst):
        g2 = g2_ref[0]
        for g in range(tb // SUBLANES):
            grp = [jnp.zeros((SUBLANES, LANES), _F32) for _ in range(2 * PACK_ROWS)]
            for k in range(SUBLANES):
                t = g * SUBLANES + k
                lhs = jnp.concatenate([lhs_ref[0, pl.ds(t * SUBLANES, SUBLANES), :],
                                       lhs_ref[1, pl.ds(t * SUBLANES, SUBLANES), :]], axis=1).astype(_BF16)
                for j in range(PACK_ROWS):
                    rows = pltpu.bitcast(src[pl.ds(t * TOK_ROWS + j, PEER_PAIRS, stride=PACK_ROWS), :], _BF16)
                    res = _dot(lhs, rows)
                    lo = res[0:1, :] + res[1:2, :]
                    hi = res[2:3, :] + res[3:4, :]
                    grp[j] = jnp.where(sub == k, lo, grp[j])
                    grp[PACK_ROWS + j] = jnp.where(sub == k, hi, grp[PACK_ROWS + j])
                _gather_token(idx_next_ref, tab_ref, dst, t)
            r0 = g * SUBLANES
            y_ref[r0:r0 + SUBLANES, :] = x_ref[r0:r0 + SUBLANES, :] + g2 * jnp.concatenate(grp, axis=1)

    _ping_pong(run, tile_a, tile_b)


def _table_spec(tab):
    return pl.BlockSpec(tab.shape, lambda i: (0, 0), pipeline_mode=pl.Buffered(1))


def _pair_specs(n_blocks):
    n_pairs = PEER_TB * PEER_PAIRS
    return [pl.BlockSpec((n_pairs,), lambda i: (i,), memory_space=pltpu.SMEM),
            pl.BlockSpec((n_pairs,), lambda i: (jnp.minimum(i + 1, n_blocks - 1),), memory_space=pltpu.SMEM)]


def _tile_scratch():
    return [pltpu.VMEM((PEER_TB * TOK_ROWS, LANES), jnp.uint32)] * 2


def _peer_u(idx, h, gate3, tab_u):
    nt, d = h.shape
    tb = PEER_TB
    return pl.pallas_call(
        _peer_u_kernel,
        grid=(nt // tb,),
        in_specs=_pair_specs(nt // tb) + [
            pl.BlockSpec((tb, d), lambda i: (i, 0)),
            pl.BlockSpec((1, PEER_PAIRS, tb), lambda i: (i, 0, 0)),
            _table_spec(tab_u),
        ],
        out_specs=pl.BlockSpec((1, PEER_PAIRS, tb), lambda i: (i, 0, 0)),
        out_shape=jax.ShapeDtypeStruct((nt // tb, PEER_PAIRS, tb), _F32),
        scratch_shapes=_tile_scratch(),
        compiler_params=_cparams(("arbitrary",)),
        name="peer_expert_in",
    )(idx, idx, h, gate3, tab_u)


def _peer_v(idx, w3, xa, mod_l, spread_even, spread_odd, tab_v, seq, n_batch):
    nt, d = xa.shape
    tb = PEER_TB

    def mrow(i):
        return jnp.minimum(i * tb // seq, n_batch)

    full = lambda a: pl.BlockSpec(a.shape, lambda i: (0,) * a.ndim)
    return pl.pallas_call(
        _peer_v_kernel,
        grid=(nt // tb,),
        in_specs=_pair_specs(nt // tb) + [
            pl.BlockSpec((1, PEER_PAIRS, tb), lambda i: (i, 0, 0)),
            pl.BlockSpec((tb, d), lambda i: (i, 0)),
            pl.BlockSpec((1, 1, d), lambda i: (mrow(i), 0, 5)),
            full(spread_even), full(spread_odd),
            _table_spec(tab_v),
        ],
        out_specs=pl.BlockSpec((tb, d), lambda i: (i, 0)),
        out_shape=jax.ShapeDtypeStruct((nt, d), _F32),
        scratch_shapes=_tile_scratch() + [pltpu.VMEM((2, tb * SUBLANES, LANES), _F32)],
        compiler_params=_cparams(("arbitrary",)),
        name="peer_expert_out",
    )(idx, idx, w3, xa, mod_l, spread_even, spread_odd, tab_v)


def _final_kernel(x_ref, g_ref, o_ref):
    x = x_ref[...]
    o_ref[...] = x * lax.rsqrt(jnp.mean(x * x, axis=-1, keepdims=True) + EPS) * g_ref[...]


def _final_norm(x, g):
    n, d = x.shape
    tb = TOK_BLOCK
    return pl.pallas_call(
        _final_kernel,
        grid=(n // tb,),
        in_specs=[pl.BlockSpec((tb, d), lambda i: (i, 0)), pl.BlockSpec((1, d), lambda i: (0, 0))],
        out_specs=pl.BlockSpec((tb, d), lambda i: (i, 0)),
        out_shape=jax.ShapeDtypeStruct((n, d), _F32),
        compiler_params=_cparams(("arbitrary",)),
        name="final_norm",
    )(x, g)


def _rope_tables(seq, dr):
    pos = jnp.arange(seq)
    rows = (pos // GRID_W).astype(_F32)
    cols = (pos % GRID_W).astype(_F32)
    h = dr // 2
    inv_freq = jnp.power(ROPE_BASE, -jnp.arange(0, h, 2, dtype=_F32) / h)
    ang_r = rows[:, None] * inv_freq[None, :]
    ang_c = cols[:, None] * inv_freq[None, :]
    cr, sr, cc, sc = jnp.cos(ang_r), jnp.sin(ang_r), jnp.cos(ang_c), jnp.sin(ang_c)
    z = jnp.zeros_like(sr)
    cos = jnp.concatenate([cr, cr, cc, cc], axis=-1)
    sa = jnp.concatenate([-sr, z, -sc, z], axis=-1)
    sb = jnp.concatenate([z, sr, z, sc], axis=-1)
    pad1 = jnp.ones((TOK_BLOCK, dr), _F32)
    pad0 = jnp.zeros((TOK_BLOCK, dr), _F32)
    return (jnp.concatenate([cos, pad1], axis=0), jnp.concatenate([sa, pad0], axis=0),
            jnp.concatenate([sb, pad0], axis=0))


def _pack_table(t):
    e, d = t.shape
    bits = lax.bitcast_convert_type(t.astype(_BF16), jnp.uint16).astype(jnp.uint32)
    packed = bits[:, :d // 2] | (bits[:, d // 2:] << 16)
    return packed.reshape(e * PACK_ROWS, LANES)


def _reorder_w_in(w_in):
    d = w_in.shape[0]
    main = jnp.concatenate([w_in[:, :384], w_in[:, 416:]], axis=1)
    kr = w_in[:, 384:416]
    return jnp.concatenate([main, kr, jnp.zeros((d, IN_COLS_PAD - COL_KR - MLA_ROPE), w_in.dtype)], axis=1)


def _heads(t, n_heads):
    b, l, _ = t.shape
    return t.reshape(b, l, n_heads, -1).transpose(0, 2, 1, 3)


def _unheads(t):
    b, h, l, d = t.shape
    return t.transpose(0, 2, 1, 3).reshape(b, l, h * d)


def kernel(x, c, ctx, c_ctx, w_mod, b_mod, norm_mix, norm_ffn, w_in, mla_q_norm, mla_kv_norm, mla_w_uq,
           mla_w_ukv, na_rpb, diff_lam_q1, diff_lam_k1, diff_lam_q2, diff_lam_k2, diff_subln, gqa_q_norm,
           gqa_k_norm, w_branch, w_gate, b_gate, w_out, peer_w_q, peer_subkeys, peer_u, peer_v, final_norm):
    bsz, seq, d = x.shape
    clen = ctx.shape[1]
    depth = w_mod.shape[0]
    n_lat, n_ctx = bsz * seq, bsz * clen
    assert d == D_MODEL and seq % TOK_BLOCK == 0 and clen % TOK_BLOCK == 0 and seq % GRID_W == 0
    assert bsz < MOD_ROWS and seq % KEY_CHUNK == 0 and seq % PEER_TB == 0 and clen % PEER_TB == 0
    blocks_per_batch = seq // TOK_BLOCK
    n_lat_blocks = n_lat // TOK_BLOCK
    tq = min(512, seq)

    cond = jnp.zeros((MOD_ROWS, d), _F32).at[:bsz].set(c).at[bsz].set(c_ctx)
    mod = _mod_all(cond, w_mod, b_mod)

    c32, sa32, sb32 = _rope_tables(seq, MLA_ROPE)
    c64, sa64, sb64 = _rope_tables(seq, GQA_HEAD_DIM)
    rows_t = seq + TOK_BLOCK
    one64 = jnp.ones((rows_t, MLA_NOPE), _F32)
    zero64 = jnp.zeros((rows_t, MLA_NOPE), _F32)
    cq = jnp.tile(jnp.concatenate([one64, c32], axis=1), (1, MLA_HEADS))
    saq = jnp.tile(jnp.concatenate([zero64, sa32], axis=1), (1, MLA_HEADS))
    sbq = jnp.tile(jnp.concatenate([zero64, sb32], axis=1), (1, MLA_HEADS))
    tabs = (cq, saq, sbq, jnp.tile(c32, (1, 8)), jnp.tile(sa32, (1, 8)), jnp.tile(sb32, (1, 8)),
            jnp.tile(c64, (1, 4)), jnp.tile(sa64, (1, 4)), jnp.tile(sb64, (1, 4)))
    seg = jnp.asarray(np.kron(np.eye(256 // GQA_HEAD_DIM), np.ones((GQA_HEAD_DIM, GQA_HEAD_DIM))), _BF16)

    pair = np.arange(PEER_PAIRS)[:, None]
    spread_even = jnp.asarray(np.arange(2 * PEER_PAIRS)[None, :] == 2 * pair, _BF16)
    spread_odd = jnp.asarray(np.arange(2 * PEER_PAIRS)[None, :] == 2 * pair + 1, _BF16)

    n_rows = seq // GRID_W
    nt = n_lat + n_ctx
    route_block = ROUTE_BLOCK if (seq % ROUTE_BLOCK == 0 and nt % ROUTE_BLOCK == 0) else TOK_BLOCK
    xa = jnp.concatenate([x.reshape(n_lat, d), ctx.reshape(n_ctx, d)], axis=0)

    for l in range(depth):
        mod_l = mod[l].reshape(MOD_ROWS, 1, N_MOD * d)
        g_mix = norm_mix[l].reshape(1, d)
        mq, mkv, mkr, na, df, gqa = _proj_prep(
            xa, mod_l, g_mix, _reorder_w_in(w_in[l]).astype(_BF16),
            mla_q_norm[l].reshape(1, -1), mla_kv_norm[l].reshape(1, -1),
            mla_w_uq[l].astype(_BF16), mla_w_ukv[l].astype(_BF16),
            jnp.tile(gqa_q_norm[l], GQA_Q_HEADS).reshape(1, -1), jnp.tile(gqa_k_norm[l], 4).reshape(1, -1),
            seg, tabs, n_lat_blocks, blocks_per_batch, bsz, blocks_per_batch)

        def both(t, n_heads):
            return (_heads(t[:n_lat].reshape(bsz, seq, -1), n_heads),
                    _heads(t[n_lat:].reshape(bsz, clen, -1), n_heads))

        def with_ones(v):
            pad = jnp.zeros(v.shape[:-1] + (LANES - HEAD_V,), v.dtype).at[..., 0].set(1)
            return jnp.concatenate([v, pad], axis=-1)

        def attend_both(q, qc, k, kc, v, vc, diff=None):
            v, vc = with_ones(v), with_ones(vc)
            return (_attention(q, k, v, kc, vc, tq, True, diff), _attention(qc, k, v, kc, vc, clen, False, diff))

        q, qc = both(mq, MLA_HEADS)
        kv, kvc = both(mkv, MLA_HEADS)
        kr, krc = both(mkr[:, :MLA_ROPE], 1)
        k = jnp.concatenate([kv[..., :MLA_NOPE], jnp.broadcast_to(kr, kv.shape[:3] + (MLA_ROPE,))], axis=-1)
        kc = jnp.concatenate([kvc[..., :MLA_NOPE], jnp.broadcast_to(krc, kvc.shape[:3] + (MLA_ROPE,))], axis=-1)
        o_a, o_a_c = attend_both(q, qc, k, kc, kv[..., MLA_NOPE:], kvc[..., MLA_NOPE:])

        q, qc = both(na[:, 0:256], NA_HEADS)
        k, kc = both(na[:, 256:512], NA_HEADS)
        v, vc = both(na[:, 512:768], NA_HEADS)
        v, vc = with_ones(v), with_ones(vc)
        o_b = _na_attention(q, k, v, kc, vc, _na_bias_table(na_rpb[l], n_rows))
        o_b_c = _attention(qc, k, v, kc, vc, clen, False)

        q, qc = both(df[:, 0:256], DIFF_HEADS)
        k, kc = both(df[:, 256:512], DIFF_HEADS)
        v, vc = both(df[:, 512:768], DIFF_HEADS)
        lam_vecs = jnp.stack([diff_lam_q1[l], diff_lam_k1[l], diff_lam_q2[l], diff_lam_k2[l]], axis=0)
        lam_init = 0.8 - 0.6 * math.exp(-0.3 * l)
        o_c, o_c_c = attend_both(q, qc, k, kc, v, vc, (lam_vecs, diff_subln[l].reshape(1, -1), lam_init))

        q, qc = both(gqa[:, 0:256], GQA_Q_HEADS)
        k, kc = both(gqa[:, 256:384], GQA_KV_HEADS)
        v, vc = both(gqa[:, 384:512], GQA_KV_HEADS)
        o_d, o_d_c = attend_both(q, qc, k, kc, v, vc)

        o_lat = jnp.concatenate([_unheads(o) for o in (o_a, o_b, o_c, o_d)], axis=-1).reshape(n_lat, d)
        o_ctx = jnp.concatenate([_unheads(o) for o in (o_a_c, o_b_c, o_c_c, o_d_c)], axis=-1).reshape(n_ctx, d)
        o_cat = jnp.concatenate([o_lat, o_ctx], axis=0)

        xa = _merge(xa, mod_l, g_mix, o_cat, w_gate[l].astype(_BF16), b_gate[l].reshape(4, 1, d),
                    w_branch[l].astype(_BF16), w_out[l].astype(_BF16), blocks_per_batch, bsz)

        wq_hi, wq_lo = _split_bf16(peer_w_q[l])
        sk = peer_subkeys[l].reshape(2 * PEER_HEADS, PEER_NKEYS, PEER_QDIM // 2)
        sk_hi, sk_lo = _split_bf16(sk)
        h2, experts, gates = _peer_route(xa, mod_l, norm_ffn[l].reshape(1, d), wq_hi, wq_lo, sk_hi, sk_lo,
                                         route_block, seq, bsz)
        idx = (experts.reshape(PEER_PAIRS, nt).T * PACK_ROWS).reshape(-1)
        gate3 = gates.reshape(PEER_PAIRS, nt // PEER_TB, PEER_TB).transpose(1, 0, 2)
        w3 = _peer_u(idx, h2, gate3, _pack_table(peer_u[l]))
        xa = _peer_v(idx, w3, xa, mod_l, spread_even, spread_odd, _pack_table(peer_v[l]), seq, bsz)

    return _final_norm(xa[:n_lat], final_norm.reshape(1, d)).reshape(bsz, seq, d)
```

```python
import functools
import math

import numpy as np
import jax
import jax.numpy as jnp
from jax import lax
from jax.experimental import pallas as pl
from jax.experimental.pallas import tpu as pltpu

D_MODEL = 1024
GRID_W = 64
EPS = 1e-6
ROPE_BASE = 10000.0
N_MOD = 6

MLA_HEADS = 4
MLA_Q_RANK = 256
MLA_KV_RANK = 128
MLA_NOPE = 64
MLA_ROPE = 32
MLA_V = 64
MLA_QK = MLA_NOPE + MLA_ROPE

NA_HEADS = 4
NA_HEAD_DIM = 64
NA_WIN_ROWS = 8
NA_WIN_COLS = 16

DIFF_HEADS = 4
DIFF_HALF = 32
DIFF_V = 2 * DIFF_HALF

GQA_Q_HEADS = 4
GQA_KV_HEADS = 2
GQA_HEAD_DIM = 64

PEER_HEADS = 8
PEER_NKEYS = 128
PEER_EXPERTS = PEER_NKEYS * PEER_NKEYS
PEER_QDIM = 256
PEER_TOPK = 16
PEER_PAIRS = PEER_HEADS * PEER_TOPK

LANES = 128
SUBLANES = 8
VMEM_LIMIT_BYTES = 56 * 1024 * 1024

COL_QC = 0
COL_KVC = 256
COL_NA = 384
COL_DIFF = 1152
COL_GQA = 1920
COL_KR = 2432
IN_COLS_PAD = 2560
MOD_ROWS = 16

TOK_BLOCK = 256
ROUTE_BLOCK = 512
HEAD_V = 64
NA_ROWS_PER_STEP = 4
PEER_TB = 32
PACK_ROWS = 4
TOK_ROWS = PEER_PAIRS * PACK_ROWS
U_GROUP = 4

_NT = (((1,), (1,)), ((), ()))
_F32 = jnp.float32
_BF16 = jnp.bfloat16


def _cparams(sem):
    return pltpu.CompilerParams(dimension_semantics=sem, vmem_limit_bytes=VMEM_LIMIT_BYTES)


def _split_bf16(a):
    hi = a.astype(_BF16)
    lo = (a - hi.astype(_F32)).astype(_BF16)
    return hi, lo


def _dot(a, b):
    return jnp.dot(a, b, preferred_element_type=_F32)


def _norm_mod(x, g, shift, scale):
    y = x * lax.rsqrt(jnp.mean(x * x, axis=-1, keepdims=True) + EPS)
    return (y * g) * (1.0 + scale) + shift


def _mod_kernel(c_ref, w_ref, b_ref, o_ref):
    c = c_ref[...]
    s = c * (1.0 / (1.0 + jnp.exp(-c)))
    s_hi, s_lo = _split_bf16(s)
    w_hi, w_lo = _split_bf16(w_ref[0])
    o_ref[0] = _dot(s_hi, w_hi) + _dot(s_hi, w_lo) + _dot(s_lo, w_hi) + b_ref[0]


def _mod_all(cond, w_mod, b_mod):
    depth, d, n = w_mod.shape
    tn = 1536
    return pl.pallas_call(
        _mod_kernel,
        grid=(depth, n // tn),
        in_specs=[
            pl.BlockSpec((MOD_ROWS, d), lambda l, j: (0, 0)),
            pl.BlockSpec((1, d, tn), lambda l, j: (l, 0, j)),
            pl.BlockSpec((1, 1, tn), lambda l, j: (l, 0, j)),
        ],
        out_specs=pl.BlockSpec((1, MOD_ROWS, tn), lambda l, j: (l, 0, j)),
        out_shape=jax.ShapeDtypeStruct((depth, MOD_ROWS, n), _F32),
        compiler_params=_cparams(("arbitrary", "arbitrary")),
        name="adaln_mod",
    )(cond, w_mod, b_mod.reshape(depth, 1, n))


def _rope(t, cos, sa, sb, dist):
    w = t.shape[-1]
    return t * cos + pltpu.roll(t, w - dist, 1) * sa + pltpu.roll(t, dist, 1) * sb


def _seg_mean_sq(t, seg_ref, seg_len):
    hi, lo = _split_bf16(t * t)
    seg = seg_ref[...]
    return (_dot(hi, seg) + _dot(lo, seg)) * (1.0 / seg_len)


def _proj_kernel(x_ref, sh_ref, sc_ref, g_ref, win_ref, qn_ref, kvn_ref, wuq_ref, wukv_ref,
                 gq_ref, gk_ref, seg_ref, cq_ref, saq_ref, sbq_ref, c32_ref, sa32_ref, sb32_ref,
                 c64_ref, sa64_ref, sb64_ref,
                 mq_ref, mkv_ref, mkr_ref, na_ref, df_ref, gqa_ref):
    h = _norm_mod(x_ref[...], g_ref[...], sh_ref[0], sc_ref[0])
    p = _dot(h.astype(_BF16), win_ref[...])

    qc = p[:, COL_QC:COL_QC + MLA_Q_RANK]
    qn = qc * lax.rsqrt(jnp.mean(qc * qc, axis=-1, keepdims=True) + EPS) * qn_ref[...]
    q = _dot(qn.astype(_BF16), wuq_ref[...])
    mq_ref[...] = (_rope(q, cq_ref[...], saq_ref[...], sbq_ref[...], MLA_ROPE // 4) * MLA_QK ** -0.5).astype(_BF16)
    kvc = p[:, COL_KVC:COL_KVC + MLA_KV_RANK]
    kvn = kvc * lax.rsqrt(jnp.mean(kvc * kvc, axis=-1, keepdims=True) + EPS) * kvn_ref[...]
    mkv_ref[...] = _dot(kvn.astype(_BF16), wukv_ref[...]).astype(_BF16)
    kr = p[:, COL_KR:COL_KR + LANES]
    mkr_ref[...] = _rope(kr, c32_ref[:, :LANES], sa32_ref[:, :LANES], sb32_ref[:, :LANES],
                         MLA_ROPE // 4).astype(_BF16)

    na_ref[:, 0:256] = (p[:, COL_NA:COL_NA + 256] * NA_HEAD_DIM ** -0.5).astype(_BF16)
    na_ref[:, 256:768] = p[:, COL_NA + 256:COL_NA + 768].astype(_BF16)

    dq = p[:, COL_DIFF:COL_DIFF + 256]
    dk = p[:, COL_DIFF + 256:COL_DIFF + 512]
    c32, sa32, sb32 = c32_ref[...], sa32_ref[...], sb32_ref[...]
    df_ref[:, 0:256] = (_rope(dq, c32, sa32, sb32, DIFF_HALF // 4) * DIFF_HALF ** -0.5).astype(_BF16)
    df_ref[:, 256:512] = _rope(dk, c32, sa32, sb32, DIFF_HALF // 4).astype(_BF16)
    df_ref[:, 512:768] = p[:, COL_DIFF + 512:COL_DIFF + 768].astype(_BF16)

    gq = p[:, COL_GQA:COL_GQA + 256]
    gk = p[:, COL_GQA + 256:COL_GQA + 384]
    gqn = gq * lax.rsqrt(_seg_mean_sq(gq, seg_ref, GQA_HEAD_DIM) + EPS) * gq_ref[...]
    gk2 = jnp.concatenate([gk, gk], axis=-1)
    gkn = gk2 * lax.rsqrt(_seg_mean_sq(gk2, seg_ref, GQA_HEAD_DIM) + EPS) * gk_ref[...]
    c64, sa64, sb64 = c64_ref[...], sa64_ref[...], sb64_ref[...]
    gqa_ref[:, 0:256] = (_rope(gqn, c64, sa64, sb64, GQA_HEAD_DIM // 4) * GQA_HEAD_DIM ** -0.5).astype(_BF16)
    gqa_ref[:, 256:384] = _rope(gkn, c64, sa64, sb64, GQA_HEAD_DIM // 4)[:, :128].astype(_BF16)
    gqa_ref[:, 384:512] = p[:, COL_GQA + 384:COL_GQA + 512].astype(_BF16)


def _proj_prep(xa, mod_l, g_norm, win, qn, kvn, wuq, wukv, gq, gk, seg, tabs, n_lat_blocks, blocks_per_batch,
               n_batch, pos_blocks):
    nt, d = xa.shape
    tb = TOK_BLOCK

    def mrow(i):
        return jnp.minimum(i // blocks_per_batch, n_batch)

    def trow(i):
        return jnp.where(i < n_lat_blocks, i % pos_blocks, pos_blocks)

    full = lambda a: pl.BlockSpec(a.shape, lambda i: (0,) * a.ndim)
    tab_specs = [pl.BlockSpec((tb, t.shape[1]), lambda i: (trow(i), 0)) for t in tabs]
    widths = (384, 512, 128, 768, 768, 512)
    return pl.pallas_call(
        _proj_kernel,
        grid=(nt // tb,),
        in_specs=[
            pl.BlockSpec((tb, d), lambda i: (i, 0)),
            pl.BlockSpec((1, 1, d), lambda i: (mrow(i), 0, 0)),
            pl.BlockSpec((1, 1, d), lambda i: (mrow(i), 0, 1)),
            full(g_norm), full(win), full(qn), full(kvn), full(wuq), full(wukv), full(gq), full(gk), full(seg),
        ] + tab_specs,
        out_specs=[pl.BlockSpec((tb, w), lambda i: (i, 0)) for w in widths],
        out_shape=[jax.ShapeDtypeStruct((nt, w), _BF16) for w in widths],
        compiler_params=_cparams(("arbitrary",)),
        name="mix_in_proj",
    )(xa, mod_l, mod_l, g_norm, win, qn, kvn, wuq, wukv, gq, gk, seg, *tabs)


def _softmax_pv(q, k_ref, vt_ref, kc_ref, vct_ref, use_latent):
    sc = lax.dot_general(kc_ref[0, 0], q, _NT, preferred_element_type=_F32)
    m = jnp.max(sc, axis=0, keepdims=True)
    if use_latent:
        s = lax.dot_general(k_ref[0, 0], q, _NT, preferred_element_type=_F32)
        m = jnp.maximum(m, jnp.max(s, axis=0, keepdims=True))
    acc = _dot(vct_ref[0, 0], jnp.exp((sc - m).astype(_BF16)))
    if use_latent:
        acc = acc + _dot(vt_ref[0, 0], jnp.exp((s - m).astype(_BF16)))
    return acc[:HEAD_V, :] / acc[HEAD_V:HEAD_V + 1, :]


def _attn_kernel(q_ref, k_ref, vt_ref, kc_ref, vct_ref, o_ref, *, use_latent):
    o_t = _softmax_pv(q_ref[0, 0], k_ref, vt_ref, kc_ref, vct_ref, use_latent)
    o_ref[0, 0] = o_t.T.astype(o_ref.dtype)


def _diff_kernel(q_ref, k_ref, vt_ref, kc_ref, vct_ref, lam_ref, sub_ref, o_ref, *, use_latent, lam_init):
    lam = (jnp.exp(jnp.sum(lam_ref[0:1, :] * lam_ref[1:2, :], axis=-1, keepdims=True))
           - jnp.exp(jnp.sum(lam_ref[2:3, :] * lam_ref[3:4, :], axis=-1, keepdims=True)) + lam_init)
    q = q_ref[0, 0]
    first = lax.broadcasted_iota(jnp.int32, q.shape, 1) < DIFF_HALF
    zero = jnp.zeros_like(q)
    o1 = _softmax_pv(jnp.where(first, q, zero), k_ref, vt_ref, kc_ref, vct_ref, use_latent)
    o2 = _softmax_pv(jnp.where(first, zero, q), k_ref, vt_ref, kc_ref, vct_ref, use_latent)
    o = (o1 - lam * o2).T
    y = o * lax.rsqrt(jnp.mean(o * o, axis=-1, keepdims=True) + EPS) * sub_ref[...]
    o_ref[0, 0] = (y * (1.0 - lam_init)).astype(o_ref.dtype)


def _attention(q, k, vt, kc, vct, tq, use_latent, diff=None):
    b, hq, lq, dq = q.shape
    hk, l = k.shape[1], k.shape[2]
    c = kc.shape[2]
    group = hq // hk
    kv_map = lambda bi, h, i: (bi, h // group, 0, 0)
    in_specs = [
        pl.BlockSpec((1, 1, tq, dq), lambda bi, h, i: (bi, h, i, 0)),
        pl.BlockSpec((1, 1, l, dq), kv_map),
        pl.BlockSpec((1, 1, LANES, l), kv_map),
        pl.BlockSpec((1, 1, c, dq), kv_map),
        pl.BlockSpec((1, 1, LANES, c), kv_map),
    ]
    args = [q, k, vt, kc, vct]
    if diff:
        lam_vecs, subln, lam_init = diff
        body = functools.partial(_diff_kernel, use_latent=use_latent, lam_init=lam_init)
        in_specs += [pl.BlockSpec(lam_vecs.shape, lambda bi, h, i: (0, 0)),
                     pl.BlockSpec(subln.shape, lambda bi, h, i: (0, 0))]
        args += [lam_vecs, subln]
    else:
        body = functools.partial(_attn_kernel, use_latent=use_latent)
    return pl.pallas_call(
        body,
        grid=(b, hq, lq // tq),
        in_specs=in_specs,
        out_specs=pl.BlockSpec((1, 1, tq, HEAD_V), lambda bi, h, i: (bi, h, i, 0)),
        out_shape=jax.ShapeDtypeStruct((b, hq, lq, HEAD_V), _BF16),
        compiler_params=_cparams(("arbitrary", "arbitrary", "arbitrary")),
        name="diff_attention" if diff else "softmax_attention",
    )(*args)


def _na_kernel(q_ref, k_ref, v_ref, kc_ref, vc_ref, bias_ref, o_ref, *, n_rows, kr, rows_per_step):
    for i in range(rows_per_step):
        r = pl.program_id(2) * rows_per_step + i
        rs = jnp.clip(r - kr // 2, 0, n_rows - kr)
        start = pl.multiple_of(rs * GRID_W, GRID_W)
        q = q_ref[0, 0, i * GRID_W:(i + 1) * GRID_W, :]
        k_win = k_ref[0, 0, pl.ds(start, kr * GRID_W), :]
        v_win = v_ref[0, 0, pl.ds(start, kr * GRID_W), :]
        s_loc = lax.dot_general(q, k_win, _NT, preferred_element_type=_F32) + bias_ref[0, r - rs]
        s_ctx = lax.dot_general(q, kc_ref[0, 0], _NT, preferred_element_type=_F32)
        m = jnp.maximum(jnp.max(s_loc, axis=-1, keepdims=True), jnp.max(s_ctx, axis=-1, keepdims=True))
        e_loc = jnp.exp((s_loc - m).astype(_BF16))
        e_ctx = jnp.exp((s_ctx - m).astype(_BF16))
        acc = _dot(e_loc, v_win) + _dot(e_ctx, vc_ref[0, 0])
        o_ref[0, 0, i * GRID_W:(i + 1) * GRID_W, :] = (
            acc[:, :HEAD_V] / acc[:, HEAD_V:HEAD_V + 1]).astype(o_ref.dtype)


def _na_attention(q, k, v, kc, vc, bias):
    b, h, l, d = q.shape
    c = kc.shape[2]
    n_rows = l // GRID_W
    kr = min(NA_WIN_ROWS, n_rows)
    rows_per_step = math.gcd(n_rows, NA_ROWS_PER_STEP)
    tq = rows_per_step * GRID_W
    return pl.pallas_call(
        functools.partial(_na_kernel, n_rows=n_rows, kr=kr, rows_per_step=rows_per_step),
        grid=(b, h, n_rows // rows_per_step),
        in_specs=[
            pl.BlockSpec((1, 1, tq, d), lambda bi, hh, r: (bi, hh, r, 0)),
            pl.BlockSpec((1, 1, l, d), lambda bi, hh, r: (bi, hh, 0, 0)),
            pl.BlockSpec((1, 1, l, LANES), lambda bi, hh, r: (bi, hh, 0, 0)),
            pl.BlockSpec((1, 1, c, d), lambda bi, hh, r: (bi, hh, 0, 0)),
            pl.BlockSpec((1, 1, c, LANES), lambda bi, hh, r: (bi, hh, 0, 0)),
            pl.BlockSpec((1,) + bias.shape[1:], lambda bi, hh, r: (hh, 0, 0, 0)),
        ],
        out_specs=pl.BlockSpec((1, 1, tq, d), lambda bi, hh, r: (bi, hh, r, 0)),
        out_shape=jax.ShapeDtypeStruct((b, h, l, d), _BF16),
        compiler_params=_cparams(("arbitrary", "arbitrary", "arbitrary")),
        name="neighbourhood_attention",
    )(q, k, v, kc, vc, bias)


def _na_bias_table(rpb, n_rows):
    kr = min(NA_WIN_ROWS, n_rows)
    nc = NA_WIN_COLS
    col_q = np.arange(GRID_W)
    col_start = np.clip(col_q - nc // 2, 0, GRID_W - nc)
    key_c = np.arange(GRID_W)
    inside = (key_c[None, :] >= col_start[:, None]) & (key_c[None, :] < col_start[:, None] + nc)
    col_off = key_c[None, :] - col_q[:, None] + (NA_WIN_COLS - 1)
    col_sel = (inside[:, :, None] & (col_off[:, :, None] == np.arange(2 * NA_WIN_COLS - 1))).astype(np.float32)
    row_off = np.arange(kr)[None, :] - np.arange(kr)[:, None] + (NA_WIN_ROWS - 1)
    row_sel = (row_off[:, :, None] == np.arange(2 * NA_WIN_ROWS - 1)).astype(np.float32)
    tab = jnp.einsum('hrj,dir,cxj->hdcix', rpb, row_sel, col_sel, precision=lax.Precision.HIGHEST)
    tab = jnp.where(inside[None, None, :, None, :], tab, -1e30)
    return tab.reshape(rpb.shape[0], kr, GRID_W, kr * GRID_W).astype(_F32)


def _merge_kernel(x_ref, sh_ref, sc_ref, g1_ref, g_ref, o_ref, wg_ref, bg_ref, wb_ref, wo_ref, y_ref):
    x = x_ref[...]
    hb = _norm_mod(x, g_ref[...], sh_ref[0], sc_ref[0]).astype(_BF16)
    o = o_ref[...]
    merged = None
    for i in range(4):
        gate = 1.0 / (1.0 + jnp.exp(-(_dot(hb, wg_ref[i]) + bg_ref[i])))
        term = gate * _dot(o[:, 256 * i:256 * i + 256], wb_ref[i])
        merged = term if merged is None else merged + term
    y_ref[...] = x + g1_ref[0] * _dot(merged.astype(_BF16), wo_ref[...])


def _merge(xa, mod_l, g_norm, o_cat, wg, bg, wb, wo, blocks_per_batch, n_batch):
    nt, d = xa.shape
    tb = TOK_BLOCK

    def mrow(i):
        return jnp.minimum(i // blocks_per_batch, n_batch)

    full = lambda a: pl.BlockSpec(a.shape, lambda i: (0,) * a.ndim)
    return pl.pallas_call(
        _merge_kernel,
        grid=(nt // tb,),
        in_specs=[
            pl.BlockSpec((tb, d), lambda i: (i, 0)),
            pl.BlockSpec((1, 1, d), lambda i: (mrow(i), 0, 0)),
            pl.BlockSpec((1, 1, d), lambda i: (mrow(i), 0, 1)),
            pl.BlockSpec((1, 1, d), lambda i: (mrow(i), 0, 2)),
            full(g_norm),
            pl.BlockSpec((tb, d), lambda i: (i, 0)),
            full(wg), full(bg), full(wb), full(wo),
        ],
        out_specs=pl.BlockSpec((tb, d), lambda i: (i, 0)),
        out_shape=jax.ShapeDtypeStruct((nt, d), _F32),
        compiler_params=_cparams(("arbitrary",)),
        name="branch_merge",
    )(xa, mod_l, mod_l, mod_l, g_norm, o_cat, wg, bg, wb, wo)


def _top_rows(s, payload, k):
    n = s.shape[0]
    iota = lax.broadcasted_iota(jnp.int32, s.shape, 0).astype(_F32)
    vals, pays = [], []
    for _ in range(k):
        m = jnp.max(s, axis=0, keepdims=True)
        pos = jnp.min(jnp.where(s == m, iota, float(n)), axis=0, keepdims=True)
        hit = iota == pos
        vals.append(m)
        if payload is None:
            pays.append(pos)
        else:
            pays.append(jnp.max(jnp.where(hit, payload, -1.0), axis=0, keepdims=True))
        s = jnp.where(hit, -jnp.inf, s)
    return jnp.concatenate(vals, axis=0), jnp.concatenate(pays, axis=0)


def _candidate_grid(s1, s2, combine):
    k = PEER_TOPK
    blocks = [combine(s1[0:1, :], s2)]
    blocks += [combine(s1[a:a + 1, :], s2[0:k // 2, :]) for a in range(1, k // 2)]
    blocks.append(combine(s1[k // 2:k, :], s2[0:1, :]))
    return jnp.concatenate(blocks, axis=0)


def _route_kernel(x_ref, sh_ref, sc_ref, g_ref, wqh_ref, wql_ref, skh_ref, skl_ref,
                  h_ref, e_ref, gate_ref, q_s, top_s, idx_s):
    h = _norm_mod(x_ref[...], g_ref[...], sh_ref[0], sc_ref[0])
    h_ref[...] = h.astype(_BF16)
    h_hi, h_lo = _split_bf16(h)
    q = _dot(h_hi, wqh_ref[...]) + _dot(h_hi, wql_ref[...]) + _dot(h_lo, wqh_ref[...])
    for hp in range(2 * PEER_HEADS):
        q_s[hp] = q[:, LANES * hp:LANES * (hp + 1)]

    def sub_scores(hp, c):
        q_hi, q_lo = _split_bf16(q_s[hp])
        sk_hi, sk_lo = skh_ref[hp], skl_ref[hp]
        s = (lax.dot_general(sk_hi, q_hi, _NT, preferred_element_type=_F32)
             + lax.dot_general(sk_hi, q_lo, _NT, preferred_element_type=_F32)
             + lax.dot_general(sk_lo, q_hi, _NT, preferred_element_type=_F32))
        vals, idx = _top_rows(s, None, PEER_TOPK)
        top_s[hp] = vals
        idx_s[hp] = idx
        return c

    lax.fori_loop(0, 2 * PEER_HEADS, sub_scores, 0)

    def head(hd, c):
        cand = _candidate_grid(top_s[2 * hd], top_s[2 * hd + 1], lambda a, b: a + b)
        cidx = _candidate_grid(idx_s[2 * hd], idx_s[2 * hd + 1], lambda a, b: a * float(PEER_NKEYS) + b)
        best, experts = _top_rows(cand, cidx, PEER_TOPK)
        ex = jnp.exp(best - best[0:1, :])
        gate_ref[hd] = ex / jnp.sum(ex, axis=0, keepdims=True)
        e_ref[hd] = experts.astype(jnp.int32)
        return c

    lax.fori_loop(0, PEER_HEADS, head, 0)


def _peer_route(xa, mod_l, g_norm, wq_hi, wq_lo, sk_hi, sk_lo, tb, seq, n_batch):
    nt, d = xa.shape

    def mrow(i):
        return jnp.minimum(i * tb // seq, n_batch)

    full = lambda a: pl.BlockSpec(a.shape, lambda i: (0,) * a.ndim)
    return pl.pallas_call(
        _route_kernel,
        grid=(nt // tb,),
        in_specs=[
            pl.BlockSpec((tb, d), lambda i: (i, 0)),
            pl.BlockSpec((1, 1, d), lambda i: (mrow(i), 0, 3)),
            pl.BlockSpec((1, 1, d), lambda i: (mrow(i), 0, 4)),
            full(g_norm), full(wq_hi), full(wq_lo), full(sk_hi), full(sk_lo),
        ],
        out_specs=[
            pl.BlockSpec((tb, d), lambda i: (i, 0)),
            pl.BlockSpec((PEER_HEADS, PEER_TOPK, tb), lambda i: (0, 0, i)),
            pl.BlockSpec((PEER_HEADS, PEER_TOPK, tb), lambda i: (0, 0, i)),
        ],
        out_shape=[
            jax.ShapeDtypeStruct((nt, d), _BF16),
            jax.ShapeDtypeStruct((PEER_HEADS, PEER_TOPK, nt), jnp.int32),
            jax.ShapeDtypeStruct((PEER_HEADS, PEER_TOPK, nt), _F32),
        ],
        scratch_shapes=[
            pltpu.VMEM((2 * PEER_HEADS, tb, LANES), _F32),
            pltpu.VMEM((2 * PEER_HEADS, PEER_TOPK, tb), _F32),
            pltpu.VMEM((2 * PEER_HEADS, PEER_TOPK, tb), _F32),
        ],
        compiler_params=_cparams(("arbitrary",)),
        name="peer_route",
    )(xa, mod_l, mod_l, g_norm, wq_hi, wq_lo, sk_hi, sk_lo)


def _gather_token(idx_ref, tab_ref, tile_ref, t):
    dst = t * TOK_ROWS
    if not isinstance(t, int):
        dst = pl.multiple_of(dst, TOK_ROWS)
    for p in range(PEER_PAIRS):
        row = pl.multiple_of(idx_ref[t * PEER_PAIRS + p], PACK_ROWS)
        tile_ref[pl.ds(dst + PACK_ROWS * p, PACK_ROWS), :] = tab_ref[pl.ds(row, PACK_ROWS), :]


def _first_block_gather(idx_ref, tab_ref, tile_ref, tb):
    @pl.when(pl.program_id(0) == 0)
    def _():
        def body(t, c):
            _gather_token(idx_ref, tab_ref, tile_ref, t)
            return c

        lax.fori_loop(0, tb, body, 0)


def _ping_pong(fn, tile_a, tile_b):
    parity = pl.program_id(0) % 2

    @pl.when(parity == 0)
    def _():
        fn(tile_a, tile_b)

    @pl.when(parity == 1)
    def _():
        fn(tile_b, tile_a)


def _unpack_f32(w):
    lo = pltpu.bitcast(w << 16, _F32)
    hi = pltpu.bitcast(w & jnp.uint32(0xFFFF0000), _F32)
    return lo, hi


def _peer_u_kernel(idx_ref, idx_next_ref, h_ref, gate_ref, tab_ref, w_ref, tile_a, tile_b):
    tb = h_ref.shape[0]
    _first_block_gather(idx_ref, tab_ref, tile_a, tb)
    lane = lax.broadcasted_iota(jnp.int32, (PEER_PAIRS, tb), 1)

    def run(src, dst):
        h = h_ref[...]
        a = jnp.zeros((PEER_PAIRS, tb), _F32)
        for g in range(tb // U_GROUP):
            t0 = g * U_GROUP
            los, his = [], []
            for j in range(PACK_ROWS):
                lo, hi = _unpack_f32(src[pl.ds(t0 * TOK_ROWS + j, U_GROUP * PEER_PAIRS, stride=PACK_ROWS), :])
                los.append(lo.astype(_BF16))
                his.append(hi.astype(_BF16))
            rows = jnp.concatenate(los + his, axis=1)
            acc = lax.dot_general(rows, h, _NT, preferred_element_type=_F32)
            for k in range(U_GROUP):
                a = a + jnp.where(lane == t0 + k, acc[PEER_PAIRS * k:PEER_PAIRS * (k + 1), :], 0.0)
            for k in range(U_GROUP):
                _gather_token(idx_next_ref, tab_ref, dst, t0 + k)
        act = 0.5 * a * (1.0 + lax.erf(a * (2.0 ** -0.5)))
        w_ref[0] = gate_ref[0] * act

    _ping_pong(run, tile_a, tile_b)


def _peer_v_kernel(idx_ref, idx_next_ref, w_ref, x_ref, g2_ref, pe_ref, po_ref, tab_ref, y_ref,
                   tile_a, tile_b, lhs_ref):
    tb = x_ref.shape[0]
    _first_block_gather(idx_ref, tab_ref, tile_a, tb)

    @pl.when(pl.program_id(0) == 0)
    def _():
        lhs_ref[...] = jnp.zeros(lhs_ref.shape, lhs_ref.dtype)

    w_hi, w_lo = _split_bf16(w_ref[0].T)
    v = 0
    for spread_ref in (pe_ref, po_ref):
        for part in (w_hi, w_lo):
            val = _dot(part, spread_ref[...])
            for hf in range(2):
                lhs_ref[hf, pl.ds(v, tb, stride=SUBLANES), :] = val[:, LANES * hf:LANES * (hf + 1)]
            v += 1
    sub = lax.broadcasted_iota(jnp.int32, (SUBLANES, LANES), 0)

    def run(src, dst):
        g2 = g2_ref[0]
        for g in range(tb // SUBLANES):
            grp = [jnp.zeros((SUBLANES, LANES), _F32) for _ in range(2 * PACK_ROWS)]
            for k in range(SUBLANES):
                t = g * SUBLANES + k
                lhs = jnp.concatenate([lhs_ref[0, pl.ds(t * SUBLANES, SUBLANES), :],
                                       lhs_ref[1, pl.ds(t * SUBLANES, SUBLANES), :]], axis=1).astype(_BF16)
                for j in range(PACK_ROWS):
                    rows = pltpu.bitcast(src[pl.ds(t * TOK_ROWS + j, PEER_PAIRS, stride=PACK_ROWS), :], _BF16)
                    res = _dot(lhs, rows)
                    lo = res[0:1, :] + res[1:2, :]
                    hi = res[2:3, :] + res[3:4, :]
                    grp[j] = jnp.where(sub == k, lo, grp[j])
                    grp[PACK_ROWS + j] = jnp.where(sub == k, hi, grp[PACK_ROWS + j])
                _gather_token(idx_next_ref, tab_ref, dst, t)
            r0 = g * SUBLANES
            y_ref[r0:r0 + SUBLANES, :] = x_ref[r0:r0 + SUBLANES, :] + g2 * jnp.concatenate(grp, axis=1)

    _ping_pong(run, tile_a, tile_b)


def _table_spec(tab):
    return pl.BlockSpec(tab.shape, lambda i: (0, 0), pipeline_mode=pl.Buffered(1))


def _pair_specs(n_blocks):
    n_pairs = PEER_TB * PEER_PAIRS
    return [pl.BlockSpec((n_pairs,), lambda i: (i,), memory_space=pltpu.SMEM),
            pl.BlockSpec((n_pairs,), lambda i: (jnp.minimum(i + 1, n_blocks - 1),), memory_space=pltpu.SMEM)]


def _tile_scratch():
    return [pltpu.VMEM((PEER_TB * TOK_ROWS, LANES), jnp.uint32)] * 2


def _peer_u(idx, h, gate3, tab_u):
    nt, d = h.shape
    tb = PEER_TB
    return pl.pallas_call(
        _peer_u_kernel,
        grid=(nt // tb,),
        in_specs=_pair_specs(nt // tb) + [
            pl.BlockSpec((tb, d), lambda i: (i, 0)),
            pl.BlockSpec((1, PEER_PAIRS, tb), lambda i: (i, 0, 0)),
            _table_spec(tab_u),
        ],
        out_specs=pl.BlockSpec((1, PEER_PAIRS, tb), lambda i: (i, 0, 0)),
        out_shape=jax.ShapeDtypeStruct((nt // tb, PEER_PAIRS, tb), _F32),
        scratch_shapes=_tile_scratch(),
        compiler_params=_cparams(("arbitrary",)),
        name="peer_expert_in",
    )(idx, idx, h, gate3, tab_u)


def _peer_v(idx, w3, xa, mod_l, spread_even, spread_odd, tab_v, seq, n_batch):
    nt, d = xa.shape
    tb = PEER_TB

    def mrow(i):
        return jnp.minimum(i * tb // seq, n_batch)

    full = lambda a: pl.BlockSpec(a.shape, lambda i: (0,) * a.ndim)
    return pl.pallas_call(
        _peer_v_kernel,
        grid=(nt // tb,),
        in_specs=_pair_specs(nt // tb) + [
            pl.BlockSpec((1, PEER_PAIRS, tb), lambda i: (i, 0, 0)),
            pl.BlockSpec((tb, d), lambda i: (i, 0)),
            pl.BlockSpec((1, 1, d), lambda i: (mrow(i), 0, 5)),
            full(spread_even), full(spread_odd),
            _table_spec(tab_v),
        ],
        out_specs=pl.BlockSpec((tb, d), lambda i: (i, 0)),
        out_shape=jax.ShapeDtypeStruct((nt, d), _F32),
        scratch_shapes=_tile_scratch() + [pltpu.VMEM((2, tb * SUBLANES, LANES), _F32)],
        compiler_params=_cparams(("arbitrary",)),
        name="peer_expert_out",
    )(idx, idx, w3, xa, mod_l, spread_even, spread_odd, tab_v)


def _final_kernel(x_ref, g_ref, o_ref):
    x = x_ref[...]
    o_ref[...] = x * lax.rsqrt(jnp.mean(x * x, axis=-1, keepdims=True) + EPS) * g_ref[...]


def _final_norm(x, g):
    n, d = x.shape
    tb = TOK_BLOCK
    return pl.pallas_call(
        _final_kernel,
        grid=(n // tb,),
        in_specs=[pl.BlockSpec((tb, d), lambda i: (i, 0)), pl.BlockSpec((1, d), lambda i: (0, 0))],
        out_specs=pl.BlockSpec((tb, d), lambda i: (i, 0)),
        out_shape=jax.ShapeDtypeStruct((n, d), _F32),
        compiler_params=_cparams(("arbitrary",)),
        name="final_norm",
    )(x, g)


def _rope_tables(seq, dr):
    pos = jnp.arange(seq)
    rows = (pos // GRID_W).astype(_F32)
    cols = (pos % GRID_W).astype(_F32)
    h = dr // 2
    inv_freq = jnp.power(ROPE_BASE, -jnp.arange(0, h, 2, dtype=_F32) / h)
    ang_r = rows[:, None] * inv_freq[None, :]
    ang_c = cols[:, None] * inv_freq[None, :]
    cr, sr, cc, sc = jnp.cos(ang_r), jnp.sin(ang_r), jnp.cos(ang_c), jnp.sin(ang_c)
    z = jnp.zeros_like(sr)
    cos = jnp.concatenate([cr, cr, cc, cc], axis=-1)
    sa = jnp.concatenate([-sr, z, -sc, z], axis=-1)
    sb = jnp.concatenate([z, sr, z, sc], axis=-1)
    pad1 = jnp.ones((TOK_BLOCK, dr), _F32)
    pad0 = jnp.zeros((TOK_BLOCK, dr), _F32)
    return (jnp.concatenate([cos, pad1], axis=0), jnp.concatenate([sa, pad0], axis=0),
            jnp.concatenate([sb, pad0], axis=0))


def _pack_table(t):
    e, d = t.shape
    bits = lax.bitcast_convert_type(t.astype(_BF16), jnp.uint16).astype(jnp.uint32)
    packed = bits[:, :d // 2] | (bits[:, d // 2:] << 16)
    return packed.reshape(e * PACK_ROWS, LANES)


def _reorder_w_in(w_in):
    d = w_in.shape[0]
    main = jnp.concatenate([w_in[:, :384], w_in[:, 416:]], axis=1)
    kr = w_in[:, 384:416]
    return jnp.concatenate([main, kr, jnp.zeros((d, IN_COLS_PAD - COL_KR - MLA_ROPE), w_in.dtype)], axis=1)


def _heads(t, n_heads):
    b, l, _ = t.shape
    return t.reshape(b, l, n_heads, -1).transpose(0, 2, 1, 3)


def _unheads(t):
    b, h, l, d = t.shape
    return t.transpose(0, 2, 1, 3).reshape(b, l, h * d)


def kernel(x, c, ctx, c_ctx, w_mod, b_mod, norm_mix, norm_ffn, w_in, mla_q_norm, mla_kv_norm, mla_w_uq,
           mla_w_ukv, na_rpb, diff_lam_q1, diff_lam_k1, diff_lam_q2, diff_lam_k2, diff_subln, gqa_q_norm,
           gqa_k_norm, w_branch, w_gate, b_gate, w_out, peer_w_q, peer_subkeys, peer_u, peer_v, final_norm):
    bsz, seq, d = x.shape
    clen = ctx.shape[1]
    depth = w_mod.shape[0]
    n_lat, n_ctx = bsz * seq, bsz * clen
    assert d == D_MODEL and seq % TOK_BLOCK == 0 and clen % TOK_BLOCK == 0 and seq % GRID_W == 0
    assert bsz < MOD_ROWS and seq % PEER_TB == 0 and clen % PEER_TB == 0
    blocks_per_batch = seq // TOK_BLOCK
    n_lat_blocks = n_lat // TOK_BLOCK
    tq = min(512, seq)

    cond = jnp.zeros((MOD_ROWS, d), _F32).at[:bsz].set(c).at[bsz].set(c_ctx)
    mod = _mod_all(cond, w_mod, b_mod)

    c32, sa32, sb32 = _rope_tables(seq, MLA_ROPE)
    c64, sa64, sb64 = _rope_tables(seq, GQA_HEAD_DIM)
    rows_t = seq + TOK_BLOCK
    one64 = jnp.ones((rows_t, MLA_NOPE), _F32)
    zero64 = jnp.zeros((rows_t, MLA_NOPE), _F32)
    cq = jnp.tile(jnp.concatenate([one64, c32], axis=1), (1, MLA_HEADS))
    saq = jnp.tile(jnp.concatenate([zero64, sa32], axis=1), (1, MLA_HEADS))
    sbq = jnp.tile(jnp.concatenate([zero64, sb32], axis=1), (1, MLA_HEADS))
    tabs = (cq, saq, sbq, jnp.tile(c32, (1, 8)), jnp.tile(sa32, (1, 8)), jnp.tile(sb32, (1, 8)),
            jnp.tile(c64, (1, 4)), jnp.tile(sa64, (1, 4)), jnp.tile(sb64, (1, 4)))
    seg = jnp.asarray(np.kron(np.eye(256 // GQA_HEAD_DIM), np.ones((GQA_HEAD_DIM, GQA_HEAD_DIM))), _BF16)

    pair = np.arange(PEER_PAIRS)[:, None]
    spread_even = jnp.asarray(np.arange(2 * PEER_PAIRS)[None, :] == 2 * pair, _BF16)
    spread_odd = jnp.asarray(np.arange(2 * PEER_PAIRS)[None, :] == 2 * pair + 1, _BF16)

    n_rows = seq // GRID_W
    nt = n_lat + n_ctx
    route_block = ROUTE_BLOCK if (seq % ROUTE_BLOCK == 0 and nt % ROUTE_BLOCK == 0) else TOK_BLOCK
    xa = jnp.concatenate([x.reshape(n_lat, d), ctx.reshape(n_ctx, d)], axis=0)

    for l in range(depth):
        mod_l = mod[l].reshape(MOD_ROWS, 1, N_MOD * d)
        g_mix = norm_mix[l].reshape(1, d)
        mq, mkv, mkr, na, df, gqa = _proj_prep(
            xa, mod_l, g_mix, _reorder_w_in(w_in[l]).astype(_BF16),
            mla_q_norm[l].reshape(1, -1), mla_kv_norm[l].reshape(1, -1),
            mla_w_uq[l].astype(_BF16), mla_w_ukv[l].astype(_BF16),
            jnp.tile(gqa_q_norm[l], GQA_Q_HEADS).reshape(1, -1), jnp.tile(gqa_k_norm[l], 4).reshape(1, -1),
            seg, tabs, n_lat_blocks, blocks_per_batch, bsz, blocks_per_batch)

        def both(t, n_heads):
            return (_heads(t[:n_lat].reshape(bsz, seq, -1), n_heads),
                    _heads(t[n_lat:].reshape(bsz, clen, -1), n_heads))

        def with_ones(v):
            pad = jnp.zeros(v.shape[:-1] + (LANES - HEAD_V,), v.dtype).at[..., 0].set(1)
            return jnp.concatenate([v, pad], axis=-1)

        def attend_both(q, qc, k, kc, v, vc, diff=None):
            vt, vct = with_ones(v).swapaxes(2, 3), with_ones(vc).swapaxes(2, 3)
            return (_attention(q, k, vt, kc, vct, tq, True, diff), _attention(qc, kc, vct, kc, vct, clen, False, diff))

        q, qc = both(mq, MLA_HEADS)
        kv, kvc = both(mkv, MLA_HEADS)
        kr, krc = both(mkr[:, :MLA_ROPE], 1)
        k = jnp.concatenate([kv[..., :MLA_NOPE], jnp.broadcast_to(kr, kv.shape[:3] + (MLA_ROPE,))], axis=-1)
        kc = jnp.concatenate([kvc[..., :MLA_NOPE], jnp.broadcast_to(krc, kvc.shape[:3] + (MLA_ROPE,))], axis=-1)
        o_a, o_a_c = attend_both(q, qc, k, kc, kv[..., MLA_NOPE:], kvc[..., MLA_NOPE:])

        q, qc = both(na[:, 0:256], NA_HEADS)
        k, kc = both(na[:, 256:512], NA_HEADS)
        v, vc = both(na[:, 512:768], NA_HEADS)
        v, vc = with_ones(v), with_ones(vc)
        o_b = _na_attention(q, k, v, kc, vc, _na_bias_table(na_rpb[l], n_rows))
        vct = vc.swapaxes(2, 3)
        o_b_c = _attention(qc, kc, vct, kc, vct, clen, False)

        q, qc = both(df[:, 0:256], DIFF_HEADS)
        k, kc = both(df[:, 256:512], DIFF_HEADS)
        v, vc = both(df[:, 512:768], DIFF_HEADS)
        lam_vecs = jnp.stack([diff_lam_q1[l], diff_lam_k1[l], diff_lam_q2[l], diff_lam_k2[l]], axis=0)
        lam_init = 0.8 - 0.6 * math.exp(-0.3 * l)
        o_c, o_c_c = attend_both(q, qc, k, kc, v, vc, (lam_vecs, diff_subln[l].reshape(1, -1), lam_init))

        q, qc = both(gqa[:, 0:256], GQA_Q_HEADS)
        k, kc = both(gqa[:, 256:384], GQA_KV_HEADS)
        v, vc = both(gqa[:, 384:512], GQA_KV_HEADS)
        o_d, o_d_c = attend_both(q, qc, k, kc, v, vc)

        o_lat = jnp.concatenate([_unheads(o) for o in (o_a, o_b, o_c, o_d)], axis=-1).reshape(n_lat, d)
        o_ctx = jnp.concatenate([_unheads(o) for o in (o_a_c, o_b_c, o_c_c, o_d_c)], axis=-1).reshape(n_ctx, d)
        o_cat = jnp.concatenate([o_lat, o_ctx], axis=0)

        xa = _merge(xa, mod_l, g_mix, o_cat, w_gate[l].astype(_BF16), b_gate[l].reshape(4, 1, d),
                    w_branch[l].astype(_BF16), w_out[l].astype(_BF16), blocks_per_batch, bsz)

        wq_hi, wq_lo = _split_bf16(peer_w_q[l])
        sk = peer_subkeys[l].reshape(2 * PEER_HEADS, PEER_NKEYS, PEER_QDIM // 2)
        sk_hi, sk_lo = _split_bf16(sk)
        h2, experts, gates = _peer_route(xa, mod_l, norm_ffn[l].reshape(1, d), wq_hi, wq_lo, sk_hi, sk_lo,
                                         route_block, seq, bsz)
        idx = (experts.reshape(PEER_PAIRS, nt).T * PACK_ROWS).reshape(-1)
        gate3 = gates.reshape(PEER_PAIRS, nt // PEER_TB, PEER_TB).transpose(1, 0, 2)
        w3 = _peer_u(idx, h2, gate3, _pack_table(peer_u[l]))
        xa = _peer_v(idx, w3, xa, mod_l, spread_even, spread_odd, _pack_table(peer_v[l]), seq, bsz)

    return _final_norm(xa[:n_lat], final_norm.reshape(1, d)).reshape(bsz, seq, d)
```

```python
import functools
import math

import numpy as np
import jax
import jax.numpy as jnp
from jax import lax
from jax.experimental import pallas as pl
from jax.experimental.pallas import tpu as pltpu

D_MODEL = 1024
GRID_W = 64
EPS = 1e-6
ROPE_BASE = 10000.0
N_MOD = 6

MLA_HEADS = 4
MLA_Q_RANK = 256
MLA_KV_RANK = 128
MLA_NOPE = 64
MLA_ROPE = 32
MLA_V = 64
MLA_QK = MLA_NOPE + MLA_ROPE

NA_HEADS = 4
NA_HEAD_DIM = 64
NA_WIN_ROWS = 8
NA_WIN_COLS = 16

DIFF_HEADS = 4
DIFF_HALF = 32
DIFF_V = 2 * DIFF_HALF

GQA_Q_HEADS = 4
GQA_KV_HEADS = 2
GQA_HEAD_DIM = 64

PEER_HEADS = 8
PEER_NKEYS = 128
PEER_EXPERTS = PEER_NKEYS * PEER_NKEYS
PEER_QDIM = 256
PEER_TOPK = 16
PEER_PAIRS = PEER_HEADS * PEER_TOPK

LANES = 128
SUBLANES = 8
VMEM_LIMIT_BYTES = 56 * 1024 * 1024

COL_QC = 0
COL_KVC = 256
COL_NA = 384
COL_DIFF = 1152
COL_GQA = 1920
COL_KR = 2432
IN_COLS_PAD = 2560
MOD_ROWS = 16

TOK_BLOCK = 256
ROUTE_BLOCK = 512
HEAD_V = 64
NA_ROWS_PER_STEP = 4
PEER_TB = 32
PACK_ROWS = 4
TOK_ROWS = PEER_PAIRS * PACK_ROWS
U_GROUP = 2

_NT = (((1,), (1,)), ((), ()))
_F32 = jnp.float32
_BF16 = jnp.bfloat16


def _cparams(sem):
    return pltpu.CompilerParams(dimension_semantics=sem, vmem_limit_bytes=VMEM_LIMIT_BYTES)


def _split_bf16(a):
    hi = a.astype(_BF16)
    lo = (a - hi.astype(_F32)).astype(_BF16)
    return hi, lo


def _dot(a, b):
    return jnp.dot(a, b, preferred_element_type=_F32)


def _norm_mod(x, g, shift, scale):
    y = x * lax.rsqrt(jnp.mean(x * x, axis=-1, keepdims=True) + EPS)
    return (y * g) * (1.0 + scale) + shift


def _mod_kernel(c_ref, w_ref, b_ref, o_ref):
    c = c_ref[...]
    s = c * (1.0 / (1.0 + jnp.exp(-c)))
    s_hi, s_lo = _split_bf16(s)
    w_hi, w_lo = _split_bf16(w_ref[0])
    o_ref[0] = _dot(s_hi, w_hi) + _dot(s_hi, w_lo) + _dot(s_lo, w_hi) + b_ref[0]


def _mod_all(cond, w_mod, b_mod):
    depth, d, n = w_mod.shape
    tn = 1536
    return pl.pallas_call(
        _mod_kernel,
        grid=(depth, n // tn),
        in_specs=[
            pl.BlockSpec((MOD_ROWS, d), lambda l, j: (0, 0)),
            pl.BlockSpec((1, d, tn), lambda l, j: (l, 0, j)),
            pl.BlockSpec((1, 1, tn), lambda l, j: (l, 0, j)),
        ],
        out_specs=pl.BlockSpec((1, MOD_ROWS, tn), lambda l, j: (l, 0, j)),
        out_shape=jax.ShapeDtypeStruct((depth, MOD_ROWS, n), _F32),
        compiler_params=_cparams(("arbitrary", "arbitrary")),
        name="adaln_mod",
    )(cond, w_mod, b_mod.reshape(depth, 1, n))


def _rope(t, cos, sa, sb, dist):
    w = t.shape[-1]
    return t * cos + pltpu.roll(t, w - dist, 1) * sa + pltpu.roll(t, dist, 1) * sb


def _seg_mean_sq(t, seg_ref, seg_len):
    hi, lo = _split_bf16(t * t)
    seg = seg_ref[...]
    return (_dot(hi, seg) + _dot(lo, seg)) * (1.0 / seg_len)


def _proj_kernel(x_ref, sh_ref, sc_ref, g_ref, win_ref, qn_ref, kvn_ref, wuq_ref, wukv_ref,
                 gq_ref, gk_ref, seg_ref, cq_ref, saq_ref, sbq_ref, c32_ref, sa32_ref, sb32_ref,
                 c64_ref, sa64_ref, sb64_ref,
                 mq_ref, mkv_ref, mkr_ref, na_ref, df_ref, gqa_ref):
    h = _norm_mod(x_ref[...], g_ref[...], sh_ref[0], sc_ref[0])
    p = _dot(h.astype(_BF16), win_ref[...])

    qc = p[:, COL_QC:COL_QC + MLA_Q_RANK]
    qn = qc * lax.rsqrt(jnp.mean(qc * qc, axis=-1, keepdims=True) + EPS) * qn_ref[...]
    q = _dot(qn.astype(_BF16), wuq_ref[...])
    mq_ref[...] = (_rope(q, cq_ref[...], saq_ref[...], sbq_ref[...], MLA_ROPE // 4) * MLA_QK ** -0.5).astype(_BF16)
    kvc = p[:, COL_KVC:COL_KVC + MLA_KV_RANK]
    kvn = kvc * lax.rsqrt(jnp.mean(kvc * kvc, axis=-1, keepdims=True) + EPS) * kvn_ref[...]
    mkv_ref[...] = _dot(kvn.astype(_BF16), wukv_ref[...]).astype(_BF16)
    kr = p[:, COL_KR:COL_KR + LANES]
    mkr_ref[...] = _rope(kr, c32_ref[:, :LANES], sa32_ref[:, :LANES], sb32_ref[:, :LANES],
                         MLA_ROPE // 4).astype(_BF16)

    na_ref[:, 0:256] = (p[:, COL_NA:COL_NA + 256] * NA_HEAD_DIM ** -0.5).astype(_BF16)
    na_ref[:, 256:768] = p[:, COL_NA + 256:COL_NA + 768].astype(_BF16)

    dq = p[:, COL_DIFF:COL_DIFF + 256]
    dk = p[:, COL_DIFF + 256:COL_DIFF + 512]
    c32, sa32, sb32 = c32_ref[...], sa32_ref[...], sb32_ref[...]
    df_ref[:, 0:256] = (_rope(dq, c32, sa32, sb32, DIFF_HALF // 4) * DIFF_HALF ** -0.5).astype(_BF16)
    df_ref[:, 256:512] = _rope(dk, c32, sa32, sb32, DIFF_HALF // 4).astype(_BF16)
    df_ref[:, 512:768] = p[:, COL_DIFF + 512:COL_DIFF + 768].astype(_BF16)

    gq = p[:, COL_GQA:COL_GQA + 256]
    gk = p[:, COL_GQA + 256:COL_GQA + 384]
    gqn = gq * lax.rsqrt(_seg_mean_sq(gq, seg_ref, GQA_HEAD_DIM) + EPS) * gq_ref[...]
    gk2 = jnp.concatenate([gk, gk], axis=-1)
    gkn = gk2 * lax.rsqrt(_seg_mean_sq(gk2, seg_ref, GQA_HEAD_DIM) + EPS) * gk_ref[...]
    c64, sa64, sb64 = c64_ref[...], sa64_ref[...], sb64_ref[...]
    gqa_ref[:, 0:256] = (_rope(gqn, c64, sa64, sb64, GQA_HEAD_DIM // 4) * GQA_HEAD_DIM ** -0.5).astype(_BF16)
    gqa_ref[:, 256:384] = _rope(gkn, c64, sa64, sb64, GQA_HEAD_DIM // 4)[:, :128].astype(_BF16)
    gqa_ref[:, 384:512] = p[:, COL_GQA + 384:COL_GQA + 512].astype(_BF16)


def _proj_prep(xa, mod_l, g_norm, win, qn, kvn, wuq, wukv, gq, gk, seg, tabs, n_lat_blocks, blocks_per_batch,
               n_batch, pos_blocks):
    nt, d = xa.shape
    tb = TOK_BLOCK

    def mrow(i):
        return jnp.minimum(i // blocks_per_batch, n_batch)

    def trow(i):
        return jnp.where(i < n_lat_blocks, i % pos_blocks, pos_blocks)

    full = lambda a: pl.BlockSpec(a.shape, lambda i: (0,) * a.ndim)
    tab_specs = [pl.BlockSpec((tb, t.shape[1]), lambda i: (trow(i), 0)) for t in tabs]
    widths = (384, 512, 128, 768, 768, 512)
    return pl.pallas_call(
        _proj_kernel,
        grid=(nt // tb,),
        in_specs=[
            pl.BlockSpec((tb, d), lambda i: (i, 0)),
            pl.BlockSpec((1, 1, d), lambda i: (mrow(i), 0, 0)),
            pl.BlockSpec((1, 1, d), lambda i: (mrow(i), 0, 1)),
            full(g_norm), full(win), full(qn), full(kvn), full(wuq), full(wukv), full(gq), full(gk), full(seg),
        ] + tab_specs,
        out_specs=[pl.BlockSpec((tb, w), lambda i: (i, 0)) for w in widths],
        out_shape=[jax.ShapeDtypeStruct((nt, w), _BF16) for w in widths],
        compiler_params=_cparams(("arbitrary",)),
        name="mix_in_proj",
    )(xa, mod_l, mod_l, g_norm, win, qn, kvn, wuq, wukv, gq, gk, seg, *tabs)


def _softmax_pv(q, k_ref, vt_ref, kc_ref, vct_ref, use_latent):
    sc = lax.dot_general(kc_ref[0, 0], q, _NT, preferred_element_type=_F32)
    m = jnp.max(sc, axis=0, keepdims=True)
    if use_latent:
        s = lax.dot_general(k_ref[0, 0], q, _NT, preferred_element_type=_F32)
        m = jnp.maximum(m, jnp.max(s, axis=0, keepdims=True))
    acc = _dot(vct_ref[0, 0], jnp.exp((sc - m).astype(_BF16)))
    if use_latent:
        acc = acc + _dot(vt_ref[0, 0], jnp.exp((s - m).astype(_BF16)))
    return acc[:HEAD_V, :] / acc[HEAD_V:HEAD_V + 1, :]


def _attn_kernel(q_ref, k_ref, vt_ref, kc_ref, vct_ref, o_ref, *, use_latent):
    o_t = _softmax_pv(q_ref[0, 0], k_ref, vt_ref, kc_ref, vct_ref, use_latent)
    o_ref[0, 0] = o_t.T.astype(o_ref.dtype)


def _diff_kernel(q_ref, k_ref, vt_ref, kc_ref, vct_ref, lam_ref, sub_ref, o_ref, *, use_latent, lam_init):
    lam = (jnp.exp(jnp.sum(lam_ref[0:1, :] * lam_ref[1:2, :], axis=-1, keepdims=True))
           - jnp.exp(jnp.sum(lam_ref[2:3, :] * lam_ref[3:4, :], axis=-1, keepdims=True)) + lam_init)
    q = q_ref[0, 0]
    first = lax.broadcasted_iota(jnp.int32, q.shape, 1) < DIFF_HALF
    zero = jnp.zeros_like(q)
    o1 = _softmax_pv(jnp.where(first, q, zero), k_ref, vt_ref, kc_ref, vct_ref, use_latent)
    o2 = _softmax_pv(jnp.where(first, zero, q), k_ref, vt_ref, kc_ref, vct_ref, use_latent)
    o = (o1 - lam * o2).T
    y = o * lax.rsqrt(jnp.mean(o * o, axis=-1, keepdims=True) + EPS) * sub_ref[...]
    o_ref[0, 0] = (y * (1.0 - lam_init)).astype(o_ref.dtype)


def _attention(q, k, vt, kc, vct, tq, use_latent, diff=None):
    b, hq, lq, dq = q.shape
    hk, l = k.shape[1], k.shape[2]
    c = kc.shape[2]
    group = hq // hk
    kv_map = lambda bi, h, i: (bi, h // group, 0, 0)
    in_specs = [
        pl.BlockSpec((1, 1, tq, dq), lambda bi, h, i: (bi, h, i, 0)),
        pl.BlockSpec((1, 1, l, dq), kv_map),
        pl.BlockSpec((1, 1, LANES, l), kv_map),
        pl.BlockSpec((1, 1, c, dq), kv_map),
        pl.BlockSpec((1, 1, LANES, c), kv_map),
    ]
    args = [q, k, vt, kc, vct]
    if diff:
        lam_vecs, subln, lam_init = diff
        body = functools.partial(_diff_kernel, use_latent=use_latent, lam_init=lam_init)
        in_specs += [pl.BlockSpec(lam_vecs.shape, lambda bi, h, i: (0, 0)),
                     pl.BlockSpec(subln.shape, lambda bi, h, i: (0, 0))]
        args += [lam_vecs, subln]
    else:
        body = functools.partial(_attn_kernel, use_latent=use_latent)
    return pl.pallas_call(
        body,
        grid=(b, hq, lq // tq),
        in_specs=in_specs,
        out_specs=pl.BlockSpec((1, 1, tq, HEAD_V), lambda bi, h, i: (bi, h, i, 0)),
        out_shape=jax.ShapeDtypeStruct((b, hq, lq, HEAD_V), _BF16),
        compiler_params=_cparams(("arbitrary", "arbitrary", "arbitrary")),
        name="diff_attention" if diff else "softmax_attention",
    )(*args)


def _na_kernel(q_ref, k_ref, v_ref, kc_ref, vc_ref, bias_ref, o_ref, *, n_rows, kr, rows_per_step):
    for i in range(rows_per_step):
        r = pl.program_id(2) * rows_per_step + i
        rs = jnp.clip(r - kr // 2, 0, n_rows - kr)
        start = pl.multiple_of(rs * GRID_W, GRID_W)
        q = q_ref[0, 0, i * GRID_W:(i + 1) * GRID_W, :]
        k_win = k_ref[0, 0, pl.ds(start, kr * GRID_W), :]
        v_win = v_ref[0, 0, pl.ds(start, kr * GRID_W), :]
        s_loc = lax.dot_general(q, k_win, _NT, preferred_element_type=_F32) + bias_ref[0, r - rs]
        s_ctx = lax.dot_general(q, kc_ref[0, 0], _NT, preferred_element_type=_F32)
        m = jnp.maximum(jnp.max(s_loc, axis=-1, keepdims=True), jnp.max(s_ctx, axis=-1, keepdims=True))
        e_loc = jnp.exp((s_loc - m).astype(_BF16))
        e_ctx = jnp.exp((s_ctx - m).astype(_BF16))
        acc = _dot(e_loc, v_win) + _dot(e_ctx, vc_ref[0, 0])
        o_ref[0, 0, i * GRID_W:(i + 1) * GRID_W, :] = (
            acc[:, :HEAD_V] / acc[:, HEAD_V:HEAD_V + 1]).astype(o_ref.dtype)


def _na_attention(q, k, v, kc, vc, bias):
    b, h, l, d = q.shape
    c = kc.shape[2]
    n_rows = l // GRID_W
    kr = min(NA_WIN_ROWS, n_rows)
    rows_per_step = math.gcd(n_rows, NA_ROWS_PER_STEP)
    tq = rows_per_step * GRID_W
    return pl.pallas_call(
        functools.partial(_na_kernel, n_rows=n_rows, kr=kr, rows_per_step=rows_per_step),
        grid=(b, h, n_rows // rows_per_step),
        in_specs=[
            pl.BlockSpec((1, 1, tq, d), lambda bi, hh, r: (bi, hh, r, 0)),
            pl.BlockSpec((1, 1, l, d), lambda bi, hh, r: (bi, hh, 0, 0)),
            pl.BlockSpec((1, 1, l, LANES), lambda bi, hh, r: (bi, hh, 0, 0)),
            pl.BlockSpec((1, 1, c, d), lambda bi, hh, r: (bi, hh, 0, 0)),
            pl.BlockSpec((1, 1, c, LANES), lambda bi, hh, r: (bi, hh, 0, 0)),
            pl.BlockSpec((1,) + bias.shape[1:], lambda bi, hh, r: (hh, 0, 0, 0)),
        ],
        out_specs=pl.BlockSpec((1, 1, tq, d), lambda bi, hh, r: (bi, hh, r, 0)),
        out_shape=jax.ShapeDtypeStruct((b, h, l, d), _BF16),
        compiler_params=_cparams(("arbitrary", "arbitrary", "arbitrary")),
        name="neighbourhood_attention",
    )(q, k, v, kc, vc, bias)


def _na_bias_table(rpb, n_rows):
    kr = min(NA_WIN_ROWS, n_rows)
    nc = NA_WIN_COLS
    col_q = np.arange(GRID_W)
    col_start = np.clip(col_q - nc // 2, 0, GRID_W - nc)
    key_c = np.arange(GRID_W)
    inside = (key_c[None, :] >= col_start[:, None]) & (key_c[None, :] < col_start[:, None] + nc)
    col_off = key_c[None, :] - col_q[:, None] + (NA_WIN_COLS - 1)
    col_sel = (inside[:, :, None] & (col_off[:, :, None] == np.arange(2 * NA_WIN_COLS - 1))).astype(np.float32)
    row_off = np.arange(kr)[None, :] - np.arange(kr)[:, None] + (NA_WIN_ROWS - 1)
    row_sel = (row_off[:, :, None] == np.arange(2 * NA_WIN_ROWS - 1)).astype(np.float32)
    tab = jnp.einsum('hrj,dir,cxj->hdcix', rpb, row_sel, col_sel, precision=lax.Precision.HIGHEST)
    tab = jnp.where(inside[None, None, :, None, :], tab, -1e30)
    return tab.reshape(rpb.shape[0], kr, GRID_W, kr * GRID_W).astype(_F32)


def _merge_kernel(x_ref, sh_ref, sc_ref, g1_ref, g_ref, o_ref, wg_ref, bg_ref, wb_ref, wo_ref, y_ref):
    x = x_ref[...]
    hb = _norm_mod(x, g_ref[...], sh_ref[0], sc_ref[0]).astype(_BF16)
    o = o_ref[...]
    merged = None
    for i in range(4):
        gate = 1.0 / (1.0 + jnp.exp(-(_dot(hb, wg_ref[i]) + bg_ref[i])))
        term = gate * _dot(o[:, 256 * i:256 * i + 256], wb_ref[i])
        merged = term if merged is None else merged + term
    y_ref[...] = x + g1_ref[0] * _dot(merged.astype(_BF16), wo_ref[...])


def _merge(xa, mod_l, g_norm, o_cat, wg, bg, wb, wo, tb, seq, n_batch):
    nt, d = xa.shape

    def mrow(i):
        return jnp.minimum(i * tb // seq, n_batch)

    full = lambda a: pl.BlockSpec(a.shape, lambda i: (0,) * a.ndim)
    return pl.pallas_call(
        _merge_kernel,
        grid=(nt // tb,),
        in_specs=[
            pl.BlockSpec((tb, d), lambda i: (i, 0)),
            pl.BlockSpec((1, 1, d), lambda i: (mrow(i), 0, 0)),
            pl.BlockSpec((1, 1, d), lambda i: (mrow(i), 0, 1)),
            pl.BlockSpec((1, 1, d), lambda i: (mrow(i), 0, 2)),
            full(g_norm),
            pl.BlockSpec((tb, d), lambda i: (i, 0)),
            full(wg), full(bg), full(wb), full(wo),
        ],
        out_specs=pl.BlockSpec((tb, d), lambda i: (i, 0)),
        out_shape=jax.ShapeDtypeStruct((nt, d), _F32),
        compiler_params=_cparams(("arbitrary",)),
        name="branch_merge",
    )(xa, mod_l, mod_l, mod_l, g_norm, o_cat, wg, bg, wb, wo)


def _top_rows(s, payload, k):
    n = s.shape[0]
    iota = lax.broadcasted_iota(jnp.int32, s.shape, 0).astype(_F32)
    vals, pays = [], []
    for _ in range(k):
        m = jnp.max(s, axis=0, keepdims=True)
        pos = jnp.min(jnp.where(s == m, iota, float(n)), axis=0, keepdims=True)
        hit = iota == pos
        vals.append(m)
        if payload is None:
            pays.append(pos)
        else:
            pays.append(jnp.max(jnp.where(hit, payload, -1.0), axis=0, keepdims=True))
        s = jnp.where(hit, -jnp.inf, s)
    return jnp.concatenate(vals, axis=0), jnp.concatenate(pays, axis=0)


def _candidate_grid(s1, s2, combine):
    k = PEER_TOPK
    blocks = [combine(s1[0:1, :], s2)]
    blocks += [combine(s1[a:a + 1, :], s2[0:k // 2, :]) for a in range(1, k // 2)]
    blocks.append(combine(s1[k // 2:k, :], s2[0:1, :]))
    return jnp.concatenate(blocks, axis=0)


def _route_kernel(x_ref, sh_ref, sc_ref, g_ref, wqh_ref, wql_ref, skh_ref, skl_ref,
                  h_ref, e_ref, gate_ref, q_s, top_s, idx_s):
    h = _norm_mod(x_ref[...], g_ref[...], sh_ref[0], sc_ref[0])
    h_ref[...] = h.astype(_BF16)
    h_hi, h_lo = _split_bf16(h)
    q = _dot(h_hi, wqh_ref[...]) + _dot(h_hi, wql_ref[...]) + _dot(h_lo, wqh_ref[...])
    for hp in range(2 * PEER_HEADS):
        q_s[hp] = q[:, LANES * hp:LANES * (hp + 1)]

    def sub_scores(hp, c):
        q_hi, q_lo = _split_bf16(q_s[hp])
        sk_hi, sk_lo = skh_ref[hp], skl_ref[hp]
        s = (lax.dot_general(sk_hi, q_hi, _NT, preferred_element_type=_F32)
             + lax.dot_general(sk_hi, q_lo, _NT, preferred_element_type=_F32)
             + lax.dot_general(sk_lo, q_hi, _NT, preferred_element_type=_F32))
        vals, idx = _top_rows(s, None, PEER_TOPK)
        top_s[hp] = vals
        idx_s[hp] = idx
        return c

    lax.fori_loop(0, 2 * PEER_HEADS, sub_scores, 0)

    def head(hd, c):
        cand = _candidate_grid(top_s[2 * hd], top_s[2 * hd + 1], lambda a, b: a + b)
        cidx = _candidate_grid(idx_s[2 * hd], idx_s[2 * hd + 1], lambda a, b: a * float(PEER_NKEYS) + b)
        best, experts = _top_rows(cand, cidx, PEER_TOPK)
        ex = jnp.exp(best - best[0:1, :])
        gate_ref[hd] = ex / jnp.sum(ex, axis=0, keepdims=True)
        e_ref[hd] = experts.astype(jnp.int32)
        return c

    lax.fori_loop(0, PEER_HEADS, head, 0)


def _peer_route(xa, mod_l, g_norm, wq_hi, wq_lo, sk_hi, sk_lo, tb, seq, n_batch):
    nt, d = xa.shape

    def mrow(i):
        return jnp.minimum(i * tb // seq, n_batch)

    full = lambda a: pl.BlockSpec(a.shape, lambda i: (0,) * a.ndim)
    return pl.pallas_call(
        _route_kernel,
        grid=(nt // tb,),
        in_specs=[
            pl.BlockSpec((tb, d), lambda i: (i, 0)),
            pl.BlockSpec((1, 1, d), lambda i: (mrow(i), 0, 3)),
            pl.BlockSpec((1, 1, d), lambda i: (mrow(i), 0, 4)),
            full(g_norm), full(wq_hi), full(wq_lo), full(sk_hi), full(sk_lo),
        ],
        out_specs=[
            pl.BlockSpec((tb, d), lambda i: (i, 0)),
            pl.BlockSpec((PEER_HEADS, PEER_TOPK, tb), lambda i: (0, 0, i)),
            pl.BlockSpec((PEER_HEADS, PEER_TOPK, tb), lambda i: (0, 0, i)),
        ],
        out_shape=[
            jax.ShapeDtypeStruct((nt, d), _BF16),
            jax.ShapeDtypeStruct((PEER_HEADS, PEER_TOPK, nt), jnp.int32),
            jax.ShapeDtypeStruct((PEER_HEADS, PEER_TOPK, nt), _F32),
        ],
        scratch_shapes=[
            pltpu.VMEM((2 * PEER_HEADS, tb, LANES), _F32),
            pltpu.VMEM((2 * PEER_HEADS, PEER_TOPK, tb), _F32),
            pltpu.VMEM((2 * PEER_HEADS, PEER_TOPK, tb), _F32),
        ],
        compiler_params=_cparams(("arbitrary",)),
        name="peer_route",
    )(xa, mod_l, mod_l, g_norm, wq_hi, wq_lo, sk_hi, sk_lo)


def _gather_token(pairs_ref, tab_ref, tile_ref, t):
    dst = t * TOK_ROWS
    if not isinstance(t, int):
        dst = pl.multiple_of(dst, TOK_ROWS)
    for p in range(PEER_PAIRS):
        row = pl.multiple_of(pairs_ref[t * PEER_PAIRS + p], PACK_ROWS)
        tile_ref[pl.ds(dst + PACK_ROWS * p, PACK_ROWS), :] = tab_ref[pl.ds(row, PACK_ROWS), :]


def _pairs_copy(idx_hbm, block, pairs_ref, sem):
    return pltpu.make_async_copy(idx_hbm.at[block], pairs_ref, sem)


def _first_step(idx_hbm, tab_ref, tile_a, pairs, sems, tb):
    n_blocks = idx_hbm.shape[0]

    @pl.when(pl.program_id(0) == 0)
    def _():
        for j in range(min(2, n_blocks)):
            _pairs_copy(idx_hbm, j, pairs[j], sems.at[j]).start()
            _pairs_copy(idx_hbm, j, pairs[j], sems.at[j]).wait()

        def body(t, c):
            _gather_token(pairs[0], tab_ref, tile_a, t)
            return c

        lax.fori_loop(0, tb, body, 0)


def _ping_pong(fn, idx_hbm, tiles, pairs, sems):
    i = pl.program_id(0)
    n_blocks = idx_hbm.shape[0]
    for parity in range(2):
        cur, nxt = parity, 1 - parity

        @pl.when(i % 2 == parity)
        def _():
            @pl.when((i >= 1) & (i + 1 <= n_blocks - 1))
            def _():
                _pairs_copy(idx_hbm, i + 1, pairs[nxt], sems.at[nxt]).wait()

            @pl.when(i + 2 <= n_blocks - 1)
            def _():
                _pairs_copy(idx_hbm, i + 2, pairs[cur], sems.at[cur]).start()

            fn(tiles[cur], tiles[nxt], pairs[nxt])


def _unpack_f32(w):
    lo = pltpu.bitcast(w << 16, _F32)
    hi = pltpu.bitcast(w & jnp.uint32(0xFFFF0000), _F32)
    return lo, hi


def _peer_u_kernel(idx_hbm, h_ref, gate_ref, tab_ref, w_ref, tile_a, tile_b, pairs_a, pairs_b, sems):
    tb = h_ref.shape[0]
    tiles, pairs = (tile_a, tile_b), (pairs_a, pairs_b)
    _first_step(idx_hbm, tab_ref, tile_a, pairs, sems, tb)
    lane = lax.broadcasted_iota(jnp.int32, (PEER_PAIRS, tb), 1)

    def run(src, dst, next_pairs):
        h = h_ref[...]
        a = jnp.zeros((PEER_PAIRS, tb), _F32)
        for g in range(tb // U_GROUP):
            t0 = g * U_GROUP
            los, his = [], []
            for j in range(PACK_ROWS):
                lo, hi = _unpack_f32(src[pl.ds(t0 * TOK_ROWS + j, U_GROUP * PEER_PAIRS, stride=PACK_ROWS), :])
                los.append(lo.astype(_BF16))
                his.append(hi.astype(_BF16))
            rows = jnp.concatenate(los + his, axis=1)
            acc = lax.dot_general(rows, h, _NT, preferred_element_type=_F32)
            for k in range(U_GROUP):
                a = a + jnp.where(lane == t0 + k, acc[PEER_PAIRS * k:PEER_PAIRS * (k + 1), :], 0.0)
            for k in range(U_GROUP):
                _gather_token(next_pairs, tab_ref, dst, t0 + k)
        act = 0.5 * a * (1.0 + lax.erf(a * (2.0 ** -0.5)))
        w_ref[0] = gate_ref[0] * act

    _ping_pong(run, idx_hbm, tiles, pairs, sems)


def _peer_v_kernel(idx_hbm, w_ref, x_ref, g2_ref, pe_ref, po_ref, tab_ref, y_ref,
                   tile_a, tile_b, pairs_a, pairs_b, sems, lhs_ref):
    tb = x_ref.shape[0]
    tiles, pairs = (tile_a, tile_b), (pairs_a, pairs_b)
    _first_step(idx_hbm, tab_ref, tile_a, pairs, sems, tb)

    @pl.when(pl.program_id(0) == 0)
    def _():
        lhs_ref[...] = jnp.zeros(lhs_ref.shape, lhs_ref.dtype)

    w_hi, w_lo = _split_bf16(w_ref[0].T)
    v = 0
    for spread_ref in (pe_ref, po_ref):
        for part in (w_hi, w_lo):
            val = _dot(part, spread_ref[...])
            for hf in range(2):
                lhs_ref[hf, pl.ds(v, tb, stride=SUBLANES), :] = val[:, LANES * hf:LANES * (hf + 1)]
            v += 1
    sub = lax.broadcasted_iota(jnp.int32, (SUBLANES, LANES), 0)

    def run(src, dst, next_pairs):
        g2 = g2_ref[0]
        for g in range(tb // SUBLANES):
            grp = [jnp.zeros((SUBLANES, LANES), _F32) for _ in range(2 * PACK_ROWS)]
            for k in range(SUBLANES):
                t = g * SUBLANES + k
                lhs = jnp.concatenate([lhs_ref[0, pl.ds(t * SUBLANES, SUBLANES), :],
                                       lhs_ref[1, pl.ds(t * SUBLANES, SUBLANES), :]], axis=1).astype(_BF16)
                for j in range(PACK_ROWS):
                    rows = pltpu.bitcast(src[pl.ds(t * TOK_ROWS + j, PEER_PAIRS, stride=PACK_ROWS), :], _BF16)
                    res = _dot(lhs, rows)
                    lo = res[0:1, :] + res[1:2, :]
                    hi = res[2:3, :] + res[3:4, :]
                    grp[j] = jnp.where(sub == k, lo, grp[j])
                    grp[PACK_ROWS + j] = jnp.where(sub == k, hi, grp[PACK_ROWS + j])
                _gather_token(next_pairs, tab_ref, dst, t)
            r0 = g * SUBLANES
            y_ref[r0:r0 + SUBLANES, :] = x_ref[r0:r0 + SUBLANES, :] + g2 * jnp.concatenate(grp, axis=1)

    _ping_pong(run, idx_hbm, tiles, pairs, sems)


def _table_spec(tab):
    return pl.BlockSpec(tab.shape, lambda i: (0, 0), pipeline_mode=pl.Buffered(1))


def _gather_scratch():
    return ([pltpu.VMEM((PEER_TB * TOK_ROWS, LANES), jnp.uint32)] * 2
            + [pltpu.SMEM((PEER_TB * PEER_PAIRS,), jnp.int32)] * 2
            + [pltpu.SemaphoreType.DMA((2,))])


def _peer_u(idx, h, gate3, tab_u):
    nt, d = h.shape
    tb = PEER_TB
    return pl.pallas_call(
        _peer_u_kernel,
        grid=(nt // tb,),
        in_specs=[
            pl.BlockSpec(memory_space=pl.ANY),
            pl.BlockSpec((tb, d), lambda i: (i, 0)),
            pl.BlockSpec((1, PEER_PAIRS, tb), lambda i: (i, 0, 0)),
            _table_spec(tab_u),
        ],
        out_specs=pl.BlockSpec((1, PEER_PAIRS, tb), lambda i: (i, 0, 0)),
        out_shape=jax.ShapeDtypeStruct((nt // tb, PEER_PAIRS, tb), _F32),
        scratch_shapes=_gather_scratch(),
        compiler_params=_cparams(("arbitrary",)),
        name="peer_expert_in",
    )(idx, h, gate3, tab_u)


def _peer_v(idx, w3, xa, mod_l, spread_even, spread_odd, tab_v, seq, n_batch):
    nt, d = xa.shape
    tb = PEER_TB

    def mrow(i):
        return jnp.minimum(i * tb // seq, n_batch)

    full = lambda a: pl.BlockSpec(a.shape, lambda i: (0,) * a.ndim)
    return pl.pallas_call(
        _peer_v_kernel,
        grid=(nt // tb,),
        in_specs=[
            pl.BlockSpec(memory_space=pl.ANY),
            pl.BlockSpec((1, PEER_PAIRS, tb), lambda i: (i, 0, 0)),
            pl.BlockSpec((tb, d), lambda i: (i, 0)),
            pl.BlockSpec((1, 1, d), lambda i: (mrow(i), 0, 5)),
            full(spread_even), full(spread_odd),
            _table_spec(tab_v),
        ],
        out_specs=pl.BlockSpec((tb, d), lambda i: (i, 0)),
        out_shape=jax.ShapeDtypeStruct((nt, d), _F32),
        scratch_shapes=_gather_scratch() + [pltpu.VMEM((2, tb * SUBLANES, LANES), _F32)],
        compiler_params=_cparams(("arbitrary",)),
        name="peer_expert_out",
    )(idx, w3, xa, mod_l, spread_even, spread_odd, tab_v)


def _final_kernel(x_ref, g_ref, o_ref):
    x = x_ref[...]
    o_ref[...] = x * lax.rsqrt(jnp.mean(x * x, axis=-1, keepdims=True) + EPS) * g_ref[...]


def _final_norm(x, g):
    n, d = x.shape
    tb = TOK_BLOCK
    return pl.pallas_call(
        _final_kernel,
        grid=(n // tb,),
        in_specs=[pl.BlockSpec((tb, d), lambda i: (i, 0)), pl.BlockSpec((1, d), lambda i: (0, 0))],
        out_specs=pl.BlockSpec((tb, d), lambda i: (i, 0)),
        out_shape=jax.ShapeDtypeStruct((n, d), _F32),
        compiler_params=_cparams(("arbitrary",)),
        name="final_norm",
    )(x, g)


def _rope_tables(seq, dr):
    pos = jnp.arange(seq)
    rows = (pos // GRID_W).astype(_F32)
    cols = (pos % GRID_W).astype(_F32)
    h = dr // 2
    inv_freq = jnp.power(ROPE_BASE, -jnp.arange(0, h, 2, dtype=_F32) / h)
    ang_r = rows[:, None] * inv_freq[None, :]
    ang_c = cols[:, None] * inv_freq[None, :]
    cr, sr, cc, sc = jnp.cos(ang_r), jnp.sin(ang_r), jnp.cos(ang_c), jnp.sin(ang_c)
    z = jnp.zeros_like(sr)
    cos = jnp.concatenate([cr, cr, cc, cc], axis=-1)
    sa = jnp.concatenate([-sr, z, -sc, z], axis=-1)
    sb = jnp.concatenate([z, sr, z, sc], axis=-1)
    pad1 = jnp.ones((TOK_BLOCK, dr), _F32)
    pad0 = jnp.zeros((TOK_BLOCK, dr), _F32)
    return (jnp.concatenate([cos, pad1], axis=0), jnp.concatenate([sa, pad0], axis=0),
            jnp.concatenate([sb, pad0], axis=0))


def _pack_table(t):
    e, d = t.shape
    bits = lax.bitcast_convert_type(t.astype(_BF16), jnp.uint16).astype(jnp.uint32)
    packed = bits[:, :d // 2] | (bits[:, d // 2:] << 16)
    return packed.reshape(e * PACK_ROWS, LANES)


def _reorder_w_in(w_in):
    d = w_in.shape[0]
    main = jnp.concatenate([w_in[:, :384], w_in[:, 416:]], axis=1)
    kr = w_in[:, 384:416]
    return jnp.concatenate([main, kr, jnp.zeros((d, IN_COLS_PAD - COL_KR - MLA_ROPE), w_in.dtype)], axis=1)


def _heads(t, n_heads):
    b, l, _ = t.shape
    return t.reshape(b, l, n_heads, -1).transpose(0, 2, 1, 3)


def _unheads(t):
    b, h, l, d = t.shape
    return t.transpose(0, 2, 1, 3).reshape(b, l, h * d)


def kernel(x, c, ctx, c_ctx, w_mod, b_mod, norm_mix, norm_ffn, w_in, mla_q_norm, mla_kv_norm, mla_w_uq,
           mla_w_ukv, na_rpb, diff_lam_q1, diff_lam_k1, diff_lam_q2, diff_lam_k2, diff_subln, gqa_q_norm,
           gqa_k_norm, w_branch, w_gate, b_gate, w_out, peer_w_q, peer_subkeys, peer_u, peer_v, final_norm):
    bsz, seq, d = x.shape
    clen = ctx.shape[1]
    depth = w_mod.shape[0]
    n_lat, n_ctx = bsz * seq, bsz * clen
    assert d == D_MODEL and seq % TOK_BLOCK == 0 and clen % TOK_BLOCK == 0 and seq % GRID_W == 0
    assert bsz < MOD_ROWS and seq % PEER_TB == 0 and clen % PEER_TB == 0
    blocks_per_batch = seq // TOK_BLOCK
    n_lat_blocks = n_lat // TOK_BLOCK
    tq = min(512, seq)

    cond = jnp.zeros((MOD_ROWS, d), _F32).at[:bsz].set(c).at[bsz].set(c_ctx)
    mod = _mod_all(cond, w_mod, b_mod)

    c32, sa32, sb32 = _rope_tables(seq, MLA_ROPE)
    c64, sa64, sb64 = _rope_tables(seq, GQA_HEAD_DIM)
    rows_t = seq + TOK_BLOCK
    one64 = jnp.ones((rows_t, MLA_NOPE), _F32)
    zero64 = jnp.zeros((rows_t, MLA_NOPE), _F32)
    cq = jnp.tile(jnp.concatenate([one64, c32], axis=1), (1, MLA_HEADS))
    saq = jnp.tile(jnp.concatenate([zero64, sa32], axis=1), (1, MLA_HEADS))
    sbq = jnp.tile(jnp.concatenate([zero64, sb32], axis=1), (1, MLA_HEADS))
    tabs = (cq, saq, sbq, jnp.tile(c32, (1, 8)), jnp.tile(sa32, (1, 8)), jnp.tile(sb32, (1, 8)),
            jnp.tile(c64, (1, 4)), jnp.tile(sa64, (1, 4)), jnp.tile(sb64, (1, 4)))
    seg = jnp.asarray(np.kron(np.eye(256 // GQA_HEAD_DIM), np.ones((GQA_HEAD_DIM, GQA_HEAD_DIM))), _BF16)

    pair = np.arange(PEER_PAIRS)[:, None]
    spread_even = jnp.asarray(np.arange(2 * PEER_PAIRS)[None, :] == 2 * pair, _BF16)
    spread_odd = jnp.asarray(np.arange(2 * PEER_PAIRS)[None, :] == 2 * pair + 1, _BF16)

    n_rows = seq // GRID_W
    nt = n_lat + n_ctx
    route_block = ROUTE_BLOCK if (seq % ROUTE_BLOCK == 0 and nt % ROUTE_BLOCK == 0) else TOK_BLOCK
    xa = jnp.concatenate([x.reshape(n_lat, d), ctx.reshape(n_ctx, d)], axis=0)

    for l in range(depth):
        mod_l = mod[l].reshape(MOD_ROWS, 1, N_MOD * d)
        g_mix = norm_mix[l].reshape(1, d)
        mq, mkv, mkr, na, df, gqa = _proj_prep(
            xa, mod_l, g_mix, _reorder_w_in(w_in[l]).astype(_BF16),
            mla_q_norm[l].reshape(1, -1), mla_kv_norm[l].reshape(1, -1),
            mla_w_uq[l].astype(_BF16), mla_w_ukv[l].astype(_BF16),
            jnp.tile(gqa_q_norm[l], GQA_Q_HEADS).reshape(1, -1), jnp.tile(gqa_k_norm[l], 4).reshape(1, -1),
            seg, tabs, n_lat_blocks, blocks_per_batch, bsz, blocks_per_batch)

        def both(t, n_heads):
            return (_heads(t[:n_lat].reshape(bsz, seq, -1), n_heads),
                    _heads(t[n_lat:].reshape(bsz, clen, -1), n_heads))

        def with_ones(v):
            pad = jnp.zeros(v.shape[:-1] + (LANES - HEAD_V,), v.dtype).at[..., 0].set(1)
            return jnp.concatenate([v, pad], axis=-1)

        def attend_both(q, qc, k, kc, v, vc, diff=None):
            vt, vct = with_ones(v).swapaxes(2, 3), with_ones(vc).swapaxes(2, 3)
            return (_attention(q, k, vt, kc, vct, tq, True, diff), _attention(qc, kc, vct, kc, vct, clen, False, diff))

        q, qc = both(mq, MLA_HEADS)
        kv, kvc = both(mkv, MLA_HEADS)
        kr, krc = both(mkr[:, :MLA_ROPE], 1)
        k = jnp.concatenate([kv[..., :MLA_NOPE], jnp.broadcast_to(kr, kv.shape[:3] + (MLA_ROPE,))], axis=-1)
        kc = jnp.concatenate([kvc[..., :MLA_NOPE], jnp.broadcast_to(krc, kvc.shape[:3] + (MLA_ROPE,))], axis=-1)
        o_a, o_a_c = attend_both(q, qc, k, kc, kv[..., MLA_NOPE:], kvc[..., MLA_NOPE:])

        q, qc = both(na[:, 0:256], NA_HEADS)
        k, kc = both(na[:, 256:512], NA_HEADS)
        v, vc = both(na[:, 512:768], NA_HEADS)
        v, vc = with_ones(v), with_ones(vc)
        o_b = _na_attention(q, k, v, kc, vc, _na_bias_table(na_rpb[l], n_rows))
        vct = vc.swapaxes(2, 3)
        o_b_c = _attention(qc, kc, vct, kc, vct, clen, False)

        q, qc = both(df[:, 0:256], DIFF_HEADS)
        k, kc = both(df[:, 256:512], DIFF_HEADS)
        v, vc = both(df[:, 512:768], DIFF_HEADS)
        lam_vecs = jnp.stack([diff_lam_q1[l], diff_lam_k1[l], diff_lam_q2[l], diff_lam_k2[l]], axis=0)
        lam_init = 0.8 - 0.6 * math.exp(-0.3 * l)
        o_c, o_c_c = attend_both(q, qc, k, kc, v, vc, (lam_vecs, diff_subln[l].reshape(1, -1), lam_init))

        q, qc = both(gqa[:, 0:256], GQA_Q_HEADS)
        k, kc = both(gqa[:, 256:384], GQA_KV_HEADS)
        v, vc = both(gqa[:, 384:512], GQA_KV_HEADS)
        o_d, o_d_c = attend_both(q, qc, k, kc, v, vc)

        o_lat = jnp.concatenate([_unheads(o) for o in (o_a, o_b, o_c, o_d)], axis=-1).reshape(n_lat, d)
        o_ctx = jnp.concatenate([_unheads(o) for o in (o_a_c, o_b_c, o_c_c, o_d_c)], axis=-1).reshape(n_ctx, d)
        o_cat = jnp.concatenate([o_lat, o_ctx], axis=0)

        xa = _merge(xa, mod_l, g_mix, o_cat, w_gate[l].astype(_BF16), b_gate[l].reshape(4, 1, d),
                    w_branch[l].astype(_BF16), w_out[l].astype(_BF16), route_block, seq, bsz)

        wq_hi, wq_lo = _split_bf16(peer_w_q[l])
        sk = peer_subkeys[l].reshape(2 * PEER_HEADS, PEER_NKEYS, PEER_QDIM // 2)
        sk_hi, sk_lo = _split_bf16(sk)
        h2, experts, gates = _peer_route(xa, mod_l, norm_ffn[l].reshape(1, d), wq_hi, wq_lo, sk_hi, sk_lo,
                                         route_block, seq, bsz)
        idx = (experts.reshape(PEER_PAIRS, nt).T * PACK_ROWS).reshape(nt // PEER_TB, PEER_TB * PEER_PAIRS)
        gate3 = gates.reshape(PEER_PAIRS, nt // PEER_TB, PEER_TB).transpose(1, 0, 2)
        w3 = _peer_u(idx, h2, gate3, _pack_table(peer_u[l]))
        xa = _peer_v(idx, w3, xa, mod_l, spread_even, spread_odd, _pack_table(peer_v[l]), seq, bsz)

    return _final_norm(xa[:n_lat], final_norm.reshape(1, d)).reshape(bsz, seq, d)
```

```python
import functools
import math

import numpy as np
import jax
import jax.numpy as jnp
from jax import lax
from jax.experimental import pallas as pl
from jax.experimental.pallas import tpu as pltpu

D_MODEL = 1024
GRID_W = 64
EPS = 1e-6
ROPE_BASE = 10000.0
N_MOD = 6

MLA_HEADS = 4
MLA_Q_RANK = 256
MLA_KV_RANK = 128
MLA_NOPE = 64
MLA_ROPE = 32
MLA_V = 64
MLA_QK = MLA_NOPE + MLA_ROPE

NA_HEADS = 4
NA_HEAD_DIM = 64
NA_WIN_ROWS = 8
NA_WIN_COLS = 16

DIFF_HEADS = 4
DIFF_HALF = 32
DIFF_V = 2 * DIFF_HALF

GQA_Q_HEADS = 4
GQA_KV_HEADS = 2
GQA_HEAD_DIM = 64

PEER_HEADS = 8
PEER_NKEYS = 128
PEER_EXPERTS = PEER_NKEYS * PEER_NKEYS
PEER_QDIM = 256
PEER_TOPK = 16
PEER_PAIRS = PEER_HEADS * PEER_TOPK

LANES = 128
SUBLANES = 8
VMEM_LIMIT_BYTES = 56 * 1024 * 1024

COL_QC = 0
COL_KVC = 256
COL_NA = 384
COL_DIFF = 1152
COL_GQA = 1920
COL_KR = 2432
IN_COLS_PAD = 2560
MOD_ROWS = 16

TOK_BLOCK = 256
ROUTE_BLOCK = 512
HEAD_V = 64
NA_ROWS_PER_STEP = 4
PEER_TB = 32
PACK_ROWS = 4
TOK_ROWS = PEER_PAIRS * PACK_ROWS
U_GROUP = 4
GQA_Q_ORDER = (0, 2, 1, 3)

_NT = (((1,), (1,)), ((), ()))
_F32 = jnp.float32
_BF16 = jnp.bfloat16


def _cparams(sem):
    return pltpu.CompilerParams(dimension_semantics=sem, vmem_limit_bytes=VMEM_LIMIT_BYTES)


def _split_bf16(a):
    hi = a.astype(_BF16)
    lo = (a - hi.astype(_F32)).astype(_BF16)
    return hi, lo


def _dot(a, b):
    return jnp.dot(a, b, preferred_element_type=_F32)


def _norm_mod(x, g, shift, scale):
    y = x * lax.rsqrt(jnp.mean(x * x, axis=-1, keepdims=True) + EPS)
    return (y * g) * (1.0 + scale) + shift


def _mod_kernel(c_ref, w_ref, b_ref, o_ref):
    c = c_ref[...]
    s = c * (1.0 / (1.0 + jnp.exp(-c)))
    s_hi, s_lo = _split_bf16(s)
    w_hi, w_lo = _split_bf16(w_ref[0])
    o_ref[0] = _dot(s_hi, w_hi) + _dot(s_hi, w_lo) + _dot(s_lo, w_hi) + b_ref[0]


def _mod_all(cond, w_mod, b_mod):
    depth, d, n = w_mod.shape
    tn = 1536
    return pl.pallas_call(
        _mod_kernel,
        grid=(depth, n // tn),
        in_specs=[
            pl.BlockSpec((MOD_ROWS, d), lambda l, j: (0, 0)),
            pl.BlockSpec((1, d, tn), lambda l, j: (l, 0, j)),
            pl.BlockSpec((1, 1, tn), lambda l, j: (l, 0, j)),
        ],
        out_specs=pl.BlockSpec((1, MOD_ROWS, tn), lambda l, j: (l, 0, j)),
        out_shape=jax.ShapeDtypeStruct((depth, MOD_ROWS, n), _F32),
        compiler_params=_cparams(("arbitrary", "arbitrary")),
        name="adaln_mod",
    )(cond, w_mod, b_mod.reshape(depth, 1, n))


def _rope(t, cos, sa, sb, dist):
    w = t.shape[-1]
    return t * cos + pltpu.roll(t, w - dist, 1) * sa + pltpu.roll(t, dist, 1) * sb


def _seg_mean_sq(t, seg_ref, seg_len):
    hi, lo = _split_bf16(t * t)
    seg = seg_ref[...]
    return (_dot(hi, seg) + _dot(lo, seg)) * (1.0 / seg_len)


def _store_values_t(vt_ref, first_head, block, upper_half_only):
    t = block.T
    tb = t.shape[1]
    ones_pad = jnp.where(lax.broadcasted_iota(jnp.int32, (LANES - HEAD_V, tb), 0) == 0, 1.0, 0.0).astype(_BF16)
    halves = [t[HEAD_V:, :]] if upper_half_only else [t[:HEAD_V, :], t[HEAD_V:, :]]
    for i, half in enumerate(halves):
        vt_ref[first_head + i, 0:HEAD_V, :] = half.astype(_BF16)
        vt_ref[first_head + i, HEAD_V:LANES, :] = ones_pad


def _proj_kernel(x_ref, sh_ref, sc_ref, g_ref, win_ref, qn_ref, kvn_ref, wuq_ref, wukv_ref,
                 gq_ref, gk_ref, seg_ref, cq_ref, saq_ref, sbq_ref, c32_ref, sa32_ref, sb32_ref,
                 c64_ref, sa64_ref, sb64_ref,
                 mq_ref, mk_ref, mvt_ref, na_ref, navt_ref, df_ref, dvt_ref, gqa_ref, gvt_ref):
    h = _norm_mod(x_ref[...], g_ref[...], sh_ref[0], sc_ref[0])
    p = _dot(h.astype(_BF16), win_ref[...])

    qc = p[:, COL_QC:COL_QC + MLA_Q_RANK]
    qn = qc * lax.rsqrt(jnp.mean(qc * qc, axis=-1, keepdims=True) + EPS) * qn_ref[...]
    q = _dot(qn.astype(_BF16), wuq_ref[...])
    mq_ref[...] = (_rope(q, cq_ref[...], saq_ref[...], sbq_ref[...], MLA_ROPE // 4) * MLA_QK ** -0.5).astype(_BF16)
    kvc = p[:, COL_KVC:COL_KVC + MLA_KV_RANK]
    kvn = kvc * lax.rsqrt(jnp.mean(kvc * kvc, axis=-1, keepdims=True) + EPS) * kvn_ref[...]
    kv = _dot(kvn.astype(_BF16), wukv_ref[...])
    kr = _rope(p[:, COL_KR:COL_KR + LANES], c32_ref[:, :LANES], sa32_ref[:, :LANES], sb32_ref[:, :LANES],
               MLA_ROPE // 4)
    lane = lax.broadcasted_iota(jnp.int32, kv.shape, 1)
    mk_ref[...] = jnp.where(lane % LANES < MLA_NOPE, kv, jnp.concatenate([kr] * MLA_HEADS, axis=1)).astype(_BF16)
    for hd in range(MLA_HEADS):
        _store_values_t(mvt_ref, hd, kv[:, LANES * hd:LANES * (hd + 1)], True)

    na_ref[:, 0:256] = (p[:, COL_NA:COL_NA + 256] * NA_HEAD_DIM ** -0.5).astype(_BF16)
    na_ref[:, 256:768] = p[:, COL_NA + 256:COL_NA + 768].astype(_BF16)
    for pr in range(NA_HEADS // 2):
        _store_values_t(navt_ref, 2 * pr, p[:, COL_NA + 512 + LANES * pr:COL_NA + 512 + LANES * (pr + 1)], False)

    dq = p[:, COL_DIFF:COL_DIFF + 256]
    dk = p[:, COL_DIFF + 256:COL_DIFF + 512]
    c32, sa32, sb32 = c32_ref[...], sa32_ref[...], sb32_ref[...]
    df_ref[:, 0:256] = (_rope(dq, c32, sa32, sb32, DIFF_HALF // 4) * DIFF_HALF ** -0.5).astype(_BF16)
    df_ref[:, 256:512] = _rope(dk, c32, sa32, sb32, DIFF_HALF // 4).astype(_BF16)
    for pr in range(DIFF_HEADS // 2):
        _store_values_t(dvt_ref, 2 * pr, p[:, COL_DIFF + 512 + LANES * pr:COL_DIFF + 512 + LANES * (pr + 1)], False)

    gq = p[:, COL_GQA:COL_GQA + 256]
    gk = p[:, COL_GQA + 256:COL_GQA + 384]
    gqn = gq * lax.rsqrt(_seg_mean_sq(gq, seg_ref, GQA_HEAD_DIM) + EPS) * gq_ref[...]
    gk2 = jnp.concatenate([gk, gk], axis=-1)
    gkn = gk2 * lax.rsqrt(_seg_mean_sq(gk2, seg_ref, GQA_HEAD_DIM) + EPS) * gk_ref[...]
    c64, sa64, sb64 = c64_ref[...], sa64_ref[...], sb64_ref[...]
    gqa_ref[:, 0:256] = (_rope(gqn, c64, sa64, sb64, GQA_HEAD_DIM // 4) * GQA_HEAD_DIM ** -0.5).astype(_BF16)
    gqa_ref[:, 256:384] = _rope(gkn, c64, sa64, sb64, GQA_HEAD_DIM // 4)[:, :128].astype(_BF16)
    _store_values_t(gvt_ref, 0, p[:, COL_GQA + 384:COL_GQA + 512], False)


def _proj_prep(xa, mod_l, g_norm, win, qn, kvn, wuq, wukv, gq, gk, seg, tabs, n_lat_blocks, blocks_per_batch,
               n_batch, pos_blocks):
    nt, d = xa.shape
    tb = TOK_BLOCK

    def mrow(i):
        return jnp.minimum(i // blocks_per_batch, n_batch)

    def trow(i):
        return jnp.where(i < n_lat_blocks, i % pos_blocks, pos_blocks)

    full = lambda a: pl.BlockSpec(a.shape, lambda i: (0,) * a.ndim)
    tab_specs = [pl.BlockSpec((tb, t.shape[1]), lambda i: (trow(i), 0)) for t in tabs]
    rows = lambda w: (pl.BlockSpec((tb, w), lambda i: (i, 0)), jax.ShapeDtypeStruct((nt, w), _BF16))
    cols = lambda nh: (pl.BlockSpec((nh, LANES, tb), lambda i: (0, 0, i)), jax.ShapeDtypeStruct((nh, LANES, nt), _BF16))
    outs = [rows(MLA_HEADS * LANES), rows(MLA_HEADS * LANES), cols(MLA_HEADS), rows(768), cols(NA_HEADS),
            rows(512), cols(DIFF_HEADS), rows(384), cols(GQA_KV_HEADS)]
    return pl.pallas_call(
        _proj_kernel,
        grid=(nt // tb,),
        in_specs=[
            pl.BlockSpec((tb, d), lambda i: (i, 0)),
            pl.BlockSpec((1, 1, d), lambda i: (mrow(i), 0, 0)),
            pl.BlockSpec((1, 1, d), lambda i: (mrow(i), 0, 1)),
            full(g_norm), full(win), full(qn), full(kvn), full(wuq), full(wukv), full(gq), full(gk), full(seg),
        ] + tab_specs,
        out_specs=[o[0] for o in outs],
        out_shape=[o[1] for o in outs],
        compiler_params=_cparams(("arbitrary",)),
        name="mix_in_proj",
    )(xa, mod_l, mod_l, g_norm, win, qn, kvn, wuq, wukv, gq, gk, seg, *tabs)


def _softmax_pv(q, k, vt, kc, vct):
    sc = lax.dot_general(kc, q, _NT, preferred_element_type=_F32)
    m = jnp.max(sc, axis=0, keepdims=True)
    if k is not None:
        s = lax.dot_general(k, q, _NT, preferred_element_type=_F32)
        m = jnp.maximum(m, jnp.max(s, axis=0, keepdims=True))
    acc = _dot(vct, jnp.exp((sc - m).astype(_BF16)))
    if k is not None:
        acc = acc + _dot(vt, jnp.exp((s - m).astype(_BF16)))
    return acc[:HEAD_V, :] / acc[HEAD_V:HEAD_V + 1, :]


def _head_operands(q_ref, k_ref, kc_ref, half, padded_heads, use_latent):
    if padded_heads:
        cols = slice(LANES * half, LANES * (half + 1))
        return q_ref[:, cols], (k_ref[:, cols] if use_latent else None), kc_ref[:, cols]
    q = q_ref[...]
    mine = (lax.broadcasted_iota(jnp.int32, q.shape, 1) // HEAD_V) == half
    return jnp.where(mine, q, jnp.zeros_like(q)), (k_ref[...] if use_latent else None), kc_ref[...]


def _attn_kernel(q_ref, k_ref, vt_ref, kc_ref, vct_ref, o_ref, *, use_latent, padded_heads):
    outs = []
    for half in range(2):
        q, k, kc = _head_operands(q_ref, k_ref, kc_ref, half, padded_heads, use_latent)
        outs.append(_softmax_pv(q, k, vt_ref[half], kc, vct_ref[half]))
    o_ref[...] = jnp.concatenate(outs, axis=0).T.astype(o_ref.dtype)


def _diff_kernel(q_ref, k_ref, vt_ref, kc_ref, vct_ref, lam_ref, sub_ref, o_ref, *, use_latent, lam_init):
    lam = (jnp.exp(jnp.sum(lam_ref[0:1, :] * lam_ref[1:2, :], axis=-1, keepdims=True))
           - jnp.exp(jnp.sum(lam_ref[2:3, :] * lam_ref[3:4, :], axis=-1, keepdims=True)) + lam_init)
    q = q_ref[...]
    k = k_ref[...] if use_latent else None
    kc = kc_ref[...]
    part = lax.broadcasted_iota(jnp.int32, q.shape, 1) // DIFF_HALF
    zero = jnp.zeros_like(q)
    outs = []
    for half in range(2):
        o1 = _softmax_pv(jnp.where(part == 2 * half, q, zero), k, vt_ref[half], kc, vct_ref[half])
        o2 = _softmax_pv(jnp.where(part == 2 * half + 1, q, zero), k, vt_ref[half], kc, vct_ref[half])
        o = o1 - lam * o2
        y = o * lax.rsqrt(jnp.mean(o * o, axis=0, keepdims=True) + EPS) * sub_ref[...]
        outs.append(y * (1.0 - lam_init))
    o_ref[...] = jnp.concatenate(outs, axis=0).T.astype(o_ref.dtype)


def _attention(src, k_src, vt_all, geom, tq, use_latent, *, q_col, k_col, kv_pairs, padded_heads=False,
               diff=None):
    bsz, seq, clen, n_lat = geom
    n_pairs = 2
    w = 2 * LANES if padded_heads else LANES
    lq = seq if use_latent else clen
    ctx0 = n_lat // clen
    q0 = 0 if use_latent else n_lat // tq
    kp = (lambda p: p) if kv_pairs > 1 else (lambda p: 0)
    in_specs = [
        pl.BlockSpec((tq, w), lambda b, p, i: (q0 + b * (lq // tq) + i, q_col + p)),
        pl.BlockSpec((seq, w), lambda b, p, i: (b, k_col + kp(p))),
        pl.BlockSpec((2, LANES, seq), lambda b, p, i: (kp(p), 0, b)),
        pl.BlockSpec((clen, w), lambda b, p, i: (ctx0 + b, k_col + kp(p))),
        pl.BlockSpec((2, LANES, clen), lambda b, p, i: (kp(p), 0, ctx0 + b)),
    ]
    args = [src, k_src, vt_all, k_src, vt_all]
    if diff:
        lam_vecs, sub_col, lam_init = diff
        body = functools.partial(_diff_kernel, use_latent=use_latent, lam_init=lam_init)
        in_specs += [pl.BlockSpec(lam_vecs.shape, lambda b, p, i: (0, 0)),
                     pl.BlockSpec(sub_col.shape, lambda b, p, i: (0, 0))]
        args += [lam_vecs, sub_col]
    else:
        body = functools.partial(_attn_kernel, use_latent=use_latent, padded_heads=padded_heads)
    return pl.pallas_call(
        body,
        grid=(bsz, n_pairs, lq // tq),
        in_specs=in_specs,
        out_specs=pl.BlockSpec((tq, LANES), lambda b, p, i: (b * (lq // tq) + i, p)),
        out_shape=jax.ShapeDtypeStruct((bsz * lq, n_pairs * LANES), _BF16),
        compiler_params=_cparams(("arbitrary", "arbitrary", "arbitrary")),
        name="diff_attention" if diff else "softmax_attention",
    )(*args)


def _na_kernel(q_ref, k_ref, v_ref, kc_ref, vc_ref, bias_ref, o_ref, *, n_rows, kr, rows_per_step):
    kc, vc = kc_ref[...], vc_ref[...]
    for i in range(rows_per_step):
        r = pl.program_id(2) * rows_per_step + i
        rs = jnp.clip(r - kr // 2, 0, n_rows - kr)
        start = pl.multiple_of(rs * GRID_W, GRID_W)
        q = q_ref[i * GRID_W:(i + 1) * GRID_W, :]
        k_win = k_ref[pl.ds(start, kr * GRID_W), :]
        v_win = v_ref[pl.ds(start, kr * GRID_W), :]
        lane = lax.broadcasted_iota(jnp.int32, q.shape, 1)
        outs = []
        for half in range(2):
            qh = jnp.where(lane // HEAD_V == half, q, jnp.zeros_like(q))
            s_loc = lax.dot_general(qh, k_win, _NT, preferred_element_type=_F32) + bias_ref[half, r - rs]
            s_ctx = lax.dot_general(qh, kc, _NT, preferred_element_type=_F32)
            m = jnp.maximum(jnp.max(s_loc, axis=-1, keepdims=True), jnp.max(s_ctx, axis=-1, keepdims=True))
            e_loc = jnp.exp((s_loc - m).astype(_BF16))
            e_ctx = jnp.exp((s_ctx - m).astype(_BF16))
            l = (jnp.sum(e_loc.astype(_F32), axis=-1, keepdims=True)
                 + jnp.sum(e_ctx.astype(_F32), axis=-1, keepdims=True))
            outs.append((_dot(e_loc, v_win) + _dot(e_ctx, vc)) / l)
        o_ref[i * GRID_W:(i + 1) * GRID_W, :] = jnp.where(lane // HEAD_V == 0, outs[0], outs[1]).astype(o_ref.dtype)


def _na_attention(src, geom, bias):
    bsz, seq, clen, n_lat = geom
    n_rows = seq // GRID_W
    kr = min(NA_WIN_ROWS, n_rows)
    rows_per_step = math.gcd(n_rows, NA_ROWS_PER_STEP)
    tq = rows_per_step * GRID_W
    ctx0 = n_lat // clen
    n_pairs = NA_HEADS // 2
    return pl.pallas_call(
        functools.partial(_na_kernel, n_rows=n_rows, kr=kr, rows_per_step=rows_per_step),
        grid=(bsz, n_pairs, n_rows // rows_per_step),
        in_specs=[
            pl.BlockSpec((tq, LANES), lambda b, p, r: (b * (seq // tq) + r, p)),
            pl.BlockSpec((seq, LANES), lambda b, p, r: (b, n_pairs + p)),
            pl.BlockSpec((seq, LANES), lambda b, p, r: (b, 2 * n_pairs + p)),
            pl.BlockSpec((clen, LANES), lambda b, p, r: (ctx0 + b, n_pairs + p)),
            pl.BlockSpec((clen, LANES), lambda b, p, r: (ctx0 + b, 2 * n_pairs + p)),
            pl.BlockSpec((2,) + bias.shape[1:], lambda b, p, r: (p, 0, 0, 0)),
        ],
        out_specs=pl.BlockSpec((tq, LANES), lambda b, p, r: (b * (seq // tq) + r, p)),
        out_shape=jax.ShapeDtypeStruct((n_lat, n_pairs * LANES), _BF16),
        compiler_params=_cparams(("arbitrary", "arbitrary", "arbitrary")),
        name="neighbourhood_attention",
    )(src, src, src, src, src, bias)


def _na_bias_table(rpb, n_rows):
    kr = min(NA_WIN_ROWS, n_rows)
    nc = NA_WIN_COLS
    col_q = np.arange(GRID_W)
    col_start = np.clip(col_q - nc // 2, 0, GRID_W - nc)
    key_c = np.arange(GRID_W)
    inside = (key_c[None, :] >= col_start[:, None]) & (key_c[None, :] < col_start[:, None] + nc)
    col_off = key_c[None, :] - col_q[:, None] + (NA_WIN_COLS - 1)
    col_sel = (inside[:, :, None] & (col_off[:, :, None] == np.arange(2 * NA_WIN_COLS - 1))).astype(np.float32)
    row_off = np.arange(kr)[None, :] - np.arange(kr)[:, None] + (NA_WIN_ROWS - 1)
    row_sel = (row_off[:, :, None] == np.arange(2 * NA_WIN_ROWS - 1)).astype(np.float32)
    tab = jnp.einsum('hrj,dir,cxj->hdcix', rpb, row_sel, col_sel, precision=lax.Precision.HIGHEST)
    tab = jnp.where(inside[None, None, :, None, :], tab, -1e30)
    return tab.reshape(rpb.shape[0], kr, GRID_W, kr * GRID_W).astype(_F32)


def _merge_kernel(x_ref, sh_ref, sc_ref, g1_ref, g_ref, oa_ref, ob_ref, oc_ref, od_ref, wg_ref, bg_ref, wb_ref, wo_ref,
                  y_ref):
    x = x_ref[...]
    hb = _norm_mod(x, g_ref[...], sh_ref[0], sc_ref[0]).astype(_BF16)
    merged = None
    for i, o_ref in enumerate((oa_ref, ob_ref, oc_ref, od_ref)):
        gate = 1.0 / (1.0 + jnp.exp(-(_dot(hb, wg_ref[i]) + bg_ref[i])))
        term = gate * _dot(o_ref[...], wb_ref[i])
        merged = term if merged is None else merged + term
    y_ref[...] = x + g1_ref[0] * _dot(merged.astype(_BF16), wo_ref[...])


def _merge(xa, mod_l, g_norm, outs, wg, bg, wb, wo, blocks_per_batch, n_batch):
    nt, d = xa.shape
    tb = TOK_BLOCK

    def mrow(i):
        return jnp.minimum(i // blocks_per_batch, n_batch)

    full = lambda a: pl.BlockSpec(a.shape, lambda i: (0,) * a.ndim)
    return pl.pallas_call(
        _merge_kernel,
        grid=(nt // tb,),
        in_specs=[
            pl.BlockSpec((tb, d), lambda i: (i, 0)),
            pl.BlockSpec((1, 1, d), lambda i: (mrow(i), 0, 0)),
            pl.BlockSpec((1, 1, d), lambda i: (mrow(i), 0, 1)),
            pl.BlockSpec((1, 1, d), lambda i: (mrow(i), 0, 2)),
            full(g_norm),
        ] + [pl.BlockSpec((tb, o.shape[1]), lambda i: (i, 0)) for o in outs] + [
            full(wg), full(bg), full(wb), full(wo),
        ],
        out_specs=pl.BlockSpec((tb, d), lambda i: (i, 0)),
        out_shape=jax.ShapeDtypeStruct((nt, d), _F32),
        compiler_params=_cparams(("arbitrary",)),
        name="branch_merge",
    )(xa, mod_l, mod_l, mod_l, g_norm, *outs, wg, bg, wb, wo)


def _top_rows(s, payload, k):
    n = s.shape[0]
    iota = lax.broadcasted_iota(jnp.int32, s.shape, 0).astype(_F32)
    vals, pays = [], []
    for _ in range(k):
        m = jnp.max(s, axis=0, keepdims=True)
        pos = jnp.min(jnp.where(s == m, iota, float(n)), axis=0, keepdims=True)
        hit = iota == pos
        vals.append(m)
        if payload is None:
            pays.append(pos)
        else:
            pays.append(jnp.max(jnp.where(hit, payload, -1.0), axis=0, keepdims=True))
        s = jnp.where(hit, -jnp.inf, s)
    return jnp.concatenate(vals, axis=0), jnp.concatenate(pays, axis=0)


def _candidate_grid(s1, s2, combine):
    k = PEER_TOPK
    blocks = [combine(s1[0:1, :], s2)]
    blocks += [combine(s1[a:a + 1, :], s2[0:k // 2, :]) for a in range(1, k // 2)]
    blocks.append(combine(s1[k // 2:k, :], s2[0:1, :]))
    return jnp.concatenate(blocks, axis=0)


def _route_kernel(x_ref, sh_ref, sc_ref, g_ref, wqh_ref, wql_ref, skh_ref, skl_ref,
                  h_ref, e_ref, gate_ref, q_s, top_s, idx_s):
    h = _norm_mod(x_ref[...], g_ref[...], sh_ref[0], sc_ref[0])
    h_ref[...] = h.astype(_BF16)
    h_hi, h_lo = _split_bf16(h)
    q = _dot(h_hi, wqh_ref[...]) + _dot(h_hi, wql_ref[...]) + _dot(h_lo, wqh_ref[...])
    for hp in range(2 * PEER_HEADS):
        q_s[hp] = q[:, LANES * hp:LANES * (hp + 1)]

    def sub_scores(hp, c):
        q_hi, q_lo = _split_bf16(q_s[hp])
        sk_hi, sk_lo = skh_ref[hp], skl_ref[hp]
        s = (lax.dot_general(sk_hi, q_hi, _NT, preferred_element_type=_F32)
             + lax.dot_general(sk_hi, q_lo, _NT, preferred_element_type=_F32)
             + lax.dot_general(sk_lo, q_hi, _NT, preferred_element_type=_F32))
        vals, idx = _top_rows(s, None, PEER_TOPK)
        top_s[hp] = vals
        idx_s[hp] = idx
        return c

    lax.fori_loop(0, 2 * PEER_HEADS, sub_scores, 0)

    def head(hd, c):
        cand = _candidate_grid(top_s[2 * hd], top_s[2 * hd + 1], lambda a, b: a + b)
        cidx = _candidate_grid(idx_s[2 * hd], idx_s[2 * hd + 1], lambda a, b: a * float(PEER_NKEYS) + b)
        best, experts = _top_rows(cand, cidx, PEER_TOPK)
        ex = jnp.exp(best - best[0:1, :])
        gate_ref[hd] = ex / jnp.sum(ex, axis=0, keepdims=True)
        e_ref[hd] = experts.astype(jnp.int32)
        return c

    lax.fori_loop(0, PEER_HEADS, head, 0)


def _peer_route(xa, mod_l, g_norm, wq_hi, wq_lo, sk_hi, sk_lo, tb, seq, n_batch):
    nt, d = xa.shape

    def mrow(i):
        return jnp.minimum(i * tb // seq, n_batch)

    full = lambda a: pl.BlockSpec(a.shape, lambda i: (0,) * a.ndim)
    return pl.pallas_call(
        _route_kernel,
        grid=(nt // tb,),
        in_specs=[
            pl.BlockSpec((tb, d), lambda i: (i, 0)),
            pl.BlockSpec((1, 1, d), lambda i: (mrow(i), 0, 3)),
            pl.BlockSpec((1, 1, d), lambda i: (mrow(i), 0, 4)),
            full(g_norm), full(wq_hi), full(wq_lo), full(sk_hi), full(sk_lo),
        ],
        out_specs=[
            pl.BlockSpec((tb, d), lambda i: (i, 0)),
            pl.BlockSpec((PEER_HEADS, PEER_TOPK, tb), lambda i: (0, 0, i)),
            pl.BlockSpec((PEER_HEADS, PEER_TOPK, tb), lambda i: (0, 0, i)),
        ],
        out_shape=[
            jax.ShapeDtypeStruct((nt, d), _BF16),
            jax.ShapeDtypeStruct((PEER_HEADS, PEER_TOPK, nt), jnp.int32),
            jax.ShapeDtypeStruct((PEER_HEADS, PEER_TOPK, nt), _F32),
        ],
        scratch_shapes=[
            pltpu.VMEM((2 * PEER_HEADS, tb, LANES), _F32),
            pltpu.VMEM((2 * PEER_HEADS, PEER_TOPK, tb), _F32),
            pltpu.VMEM((2 * PEER_HEADS, PEER_TOPK, tb), _F32),
        ],
        compiler_params=_cparams(("arbitrary",)),
        name="peer_route",
    )(xa, mod_l, mod_l, g_norm, wq_hi, wq_lo, sk_hi, sk_lo)


def _gather_token(idx_ref, tab_ref, tile_ref, t):
    dst = t * TOK_ROWS
    if not isinstance(t, int):
        dst = pl.multiple_of(dst, TOK_ROWS)
    for p in range(PEER_PAIRS):
        row = pl.multiple_of(idx_ref[t * PEER_PAIRS + p], PACK_ROWS)
        tile_ref[pl.ds(dst + PACK_ROWS * p, PACK_ROWS), :] = tab_ref[pl.ds(row, PACK_ROWS), :]


def _first_block_gather(idx_ref, tab_ref, tile_ref, tb):
    @pl.when(pl.program_id(0) == 0)
    def _():
        def body(t, c):
            _gather_token(idx_ref, tab_ref, tile_ref, t)
            return c

        lax.fori_loop(0, tb, body, 0)


def _ping_pong(fn, tile_a, tile_b):
    parity = pl.program_id(0) % 2

    @pl.when(parity == 0)
    def _():
        fn(tile_a, tile_b)

    @pl.when(parity == 1)
    def _():
        fn(tile_b, tile_a)


def _unpack_f32(w):
    lo = pltpu.bitcast(w << 16, _F32)
    hi = pltpu.bitcast(w & jnp.uint32(0xFFFF0000), _F32)
    return lo, hi


def _peer_u_kernel(idx_ref, idx_next_ref, h_ref, gate_ref, tab_ref, w_ref, tile_a, tile_b):
    tb = h_ref.shape[0]
    _first_block_gather(idx_ref, tab_ref, tile_a, tb)
    lane = lax.broadcasted_iota(jnp.int32, (PEER_PAIRS, tb), 1)

    def run(src, dst):
        h = h_ref[...]
        a = jnp.zeros((PEER_PAIRS, tb), _F32)
        for g in range(tb // U_GROUP):
            t0 = g * U_GROUP
            los, his = [], []
            for j in range(PACK_ROWS):
                lo, hi = _unpack_f32(src[pl.ds(t0 * TOK_ROWS + j, U_GROUP * PEER_PAIRS, stride=PACK_ROWS), :])
                los.append(lo.astype(_BF16))
                his.append(hi.astype(_BF16))
            rows = jnp.concatenate(los + his, axis=1)
            acc = lax.dot_general(rows, h, _NT, preferred_element_type=_F32)
            for k in range(U_GROUP):
                a = a + jnp.where(lane == t0 + k, acc[PEER_PAIRS * k:PEER_PAIRS * (k + 1), :], 0.0)
            for k in range(U_GROUP):
                _gather_token(idx_next_ref, tab_ref, dst, t0 + k)
        act = 0.5 * a * (1.0 + lax.erf(a * (2.0 ** -0.5)))
        w_ref[0] = gate_ref[0] * act

    _ping_pong(run, tile_a, tile_b)


def _peer_v_kernel(idx_ref, idx_next_ref, w_ref, x_ref, g2_ref, pe_ref, po_ref, tab_ref, y_ref,
                   tile_a, tile_b, lhs_ref):
    tb = x_ref.shape[0]
    _first_block_gather(idx_ref, tab_ref, tile_a, tb)

    @pl.when(pl.program_id(0) == 0)
    def _():
        lhs_ref[...] = jnp.zeros(lhs_ref.shape, lhs_ref.dtype)

    w_hi, w_lo = _split_bf16(w_ref[0].T)
    v = 0
    for spread_ref in (pe_ref, po_ref):
        for part in (w_hi, w_lo):
            val = _dot(part, spread_ref[...])
            for hf in range(2):
                lhs_ref[hf, pl.ds(v, tb, stride=SUBLANES), :] = val[:, LANES * hf:LANES * (hf + 1)]
            v += 1
    sub = lax.broadcasted_iota(jnp.int32, (SUBLANES, LANES), 0)

    def run(src, dst):
        g2 = g2_ref[0]
        for g in range(tb // SUBLANES):
            grp = [jnp.zeros((SUBLANES, LANES), _F32) for _ in range(2 * PACK_ROWS)]
            for k in range(SUBLANES):
                t = g * SUBLANES + k
                lhs = jnp.concatenate([lhs_ref[0, pl.ds(t * SUBLANES, SUBLANES), :],
                                       lhs_ref[1, pl.ds(t * SUBLANES, SUBLANES), :]], axis=1).astype(_BF16)
                for j in range(PACK_ROWS):
                    rows = pltpu.bitcast(src[pl.ds(t * TOK_ROWS + j, PEER_PAIRS, stride=PACK_ROWS), :], _BF16)
                    res = _dot(lhs, rows)
                    lo = res[0:1, :] + res[1:2, :]
                    hi = res[2:3, :] + res[3:4, :]
                    grp[j] = jnp.where(sub == k, lo, grp[j])
                    grp[PACK_ROWS + j] = jnp.where(sub == k, hi, grp[PACK_ROWS + j])
                _gather_token(idx_next_ref, tab_ref, dst, t)
            r0 = g * SUBLANES
            y_ref[r0:r0 + SUBLANES, :] = x_ref[r0:r0 + SUBLANES, :] + g2 * jnp.concatenate(grp, axis=1)

    _ping_pong(run, tile_a, tile_b)


def _table_spec(tab):
    return pl.BlockSpec(tab.shape, lambda i: (0, 0), pipeline_mode=pl.Buffered(1))


def _pair_specs(n_blocks):
    n_pairs = PEER_TB * PEER_PAIRS
    return [pl.BlockSpec((n_pairs,), lambda i: (i,), memory_space=pltpu.SMEM),
            pl.BlockSpec((n_pairs,), lambda i: (jnp.minimum(i + 1, n_blocks - 1),), memory_space=pltpu.SMEM)]


def _tile_scratch():
    return [pltpu.VMEM((PEER_TB * TOK_ROWS, LANES), jnp.uint32)] * 2


def _peer_u(idx, h, gate3, tab_u):
    nt, d = h.shape
    tb = PEER_TB
    return pl.pallas_call(
        _peer_u_kernel,
        grid=(nt // tb,),
        in_specs=_pair_specs(nt // tb) + [
            pl.BlockSpec((tb, d), lambda i: (i, 0)),
            pl.BlockSpec((1, PEER_PAIRS, tb), lambda i: (i, 0, 0)),
            _table_spec(tab_u),
        ],
        out_specs=pl.BlockSpec((1, PEER_PAIRS, tb), lambda i: (i, 0, 0)),
        out_shape=jax.ShapeDtypeStruct((nt // tb, PEER_PAIRS, tb), _F32),
        scratch_shapes=_tile_scratch(),
        compiler_params=_cparams(("arbitrary",)),
        name="peer_expert_in",
    )(idx, idx, h, gate3, tab_u)


def _peer_v(idx, w3, xa, mod_l, spread_even, spread_odd, tab_v, seq, n_batch):
    nt, d = xa.shape
    tb = PEER_TB

    def mrow(i):
        return jnp.minimum(i * tb // seq, n_batch)

    full = lambda a: pl.BlockSpec(a.shape, lambda i: (0,) * a.ndim)
    return pl.pallas_call(
        _peer_v_kernel,
        grid=(nt // tb,),
        in_specs=_pair_specs(nt // tb) + [
            pl.BlockSpec((1, PEER_PAIRS, tb), lambda i: (i, 0, 0)),
            pl.BlockSpec((tb, d), lambda i: (i, 0)),
            pl.BlockSpec((1, 1, d), lambda i: (mrow(i), 0, 5)),
            full(spread_even), full(spread_odd),
            _table_spec(tab_v),
        ],
        out_specs=pl.BlockSpec((tb, d), lambda i: (i, 0)),
        out_shape=jax.ShapeDtypeStruct((nt, d), _F32),
        scratch_shapes=_tile_scratch() + [pltpu.VMEM((2, tb * SUBLANES, LANES), _F32)],
        compiler_params=_cparams(("arbitrary",)),
        name="peer_expert_out",
    )(idx, idx, w3, xa, mod_l, spread_even, spread_odd, tab_v)


def _final_kernel(x_ref, g_ref, o_ref):
    x = x_ref[...]
    o_ref[...] = x * lax.rsqrt(jnp.mean(x * x, axis=-1, keepdims=True) + EPS) * g_ref[...]


def _final_norm(x, g):
    n, d = x.shape
    tb = TOK_BLOCK
    return pl.pallas_call(
        _final_kernel,
        grid=(n // tb,),
        in_specs=[pl.BlockSpec((tb, d), lambda i: (i, 0)), pl.BlockSpec((1, d), lambda i: (0, 0))],
        out_specs=pl.BlockSpec((tb, d), lambda i: (i, 0)),
        out_shape=jax.ShapeDtypeStruct((n, d), _F32),
        compiler_params=_cparams(("arbitrary",)),
        name="final_norm",
    )(x, g)


def _rope_tables(seq, dr):
    pos = jnp.arange(seq)
    rows = (pos // GRID_W).astype(_F32)
    cols = (pos % GRID_W).astype(_F32)
    h = dr // 2
    inv_freq = jnp.power(ROPE_BASE, -jnp.arange(0, h, 2, dtype=_F32) / h)
    ang_r = rows[:, None] * inv_freq[None, :]
    ang_c = cols[:, None] * inv_freq[None, :]
    cr, sr, cc, sc = jnp.cos(ang_r), jnp.sin(ang_r), jnp.cos(ang_c), jnp.sin(ang_c)
    z = jnp.zeros_like(sr)
    cos = jnp.concatenate([cr, cr, cc, cc], axis=-1)
    sa = jnp.concatenate([-sr, z, -sc, z], axis=-1)
    sb = jnp.concatenate([z, sr, z, sc], axis=-1)
    pad1 = jnp.ones((TOK_BLOCK, dr), _F32)
    pad0 = jnp.zeros((TOK_BLOCK, dr), _F32)
    return (jnp.concatenate([cos, pad1], axis=0), jnp.concatenate([sa, pad0], axis=0),
            jnp.concatenate([sb, pad0], axis=0))


def _pack_table(t):
    e, d = t.shape
    bits = lax.bitcast_convert_type(t.astype(_BF16), jnp.uint16).astype(jnp.uint32)
    packed = bits[:, :d // 2] | (bits[:, d // 2:] << 16)
    return packed.reshape(e * PACK_ROWS, LANES)


def _reorder_w_in(w_in):
    d = w_in.shape[0]
    gq0 = 416 + 768 + 768
    gq = w_in[:, gq0:gq0 + 256].reshape(d, GQA_Q_HEADS, GQA_HEAD_DIM)[:, jnp.array(GQA_Q_ORDER), :].reshape(d, 256)
    zeros = lambda n: jnp.zeros((d, n), w_in.dtype)
    return jnp.concatenate([w_in[:, :384], w_in[:, 416:gq0], gq, w_in[:, gq0 + 256:],
                            zeros(MLA_NOPE), w_in[:, 384:416], zeros(LANES - MLA_NOPE - MLA_ROPE)], axis=1)


def _pad_heads(w, n_heads, width):
    k = w.shape[0]
    w = w.reshape(k, n_heads, width)
    return jnp.concatenate([w, jnp.zeros((k, n_heads, LANES - width), w.dtype)], axis=-1).reshape(k, n_heads * LANES)


def kernel(x, c, ctx, c_ctx, w_mod, b_mod, norm_mix, norm_ffn, w_in, mla_q_norm, mla_kv_norm, mla_w_uq,
           mla_w_ukv, na_rpb, diff_lam_q1, diff_lam_k1, diff_lam_q2, diff_lam_k2, diff_subln, gqa_q_norm,
           gqa_k_norm, w_branch, w_gate, b_gate, w_out, peer_w_q, peer_subkeys, peer_u, peer_v, final_norm):
    bsz, seq, d = x.shape
    clen = ctx.shape[1]
    depth = w_mod.shape[0]
    n_lat, n_ctx = bsz * seq, bsz * clen
    assert d == D_MODEL and seq % TOK_BLOCK == 0 and clen % TOK_BLOCK == 0 and seq % GRID_W == 0
    assert bsz < MOD_ROWS and seq % PEER_TB == 0 and clen % PEER_TB == 0
    blocks_per_batch = seq // TOK_BLOCK
    n_lat_blocks = n_lat // TOK_BLOCK
    tq = min(512, seq)

    cond = jnp.zeros((MOD_ROWS, d), _F32).at[:bsz].set(c).at[bsz].set(c_ctx)
    mod = _mod_all(cond, w_mod, b_mod)

    c32, sa32, sb32 = _rope_tables(seq, MLA_ROPE)
    c64, sa64, sb64 = _rope_tables(seq, GQA_HEAD_DIM)
    rows_t = seq + TOK_BLOCK
    one64 = jnp.ones((rows_t, MLA_NOPE), _F32)
    zero64 = jnp.zeros((rows_t, MLA_NOPE), _F32)
    one32 = jnp.ones((rows_t, LANES - MLA_QK), _F32)
    zero32 = jnp.zeros((rows_t, LANES - MLA_QK), _F32)
    cq = jnp.tile(jnp.concatenate([one64, c32, one32], axis=1), (1, MLA_HEADS))
    saq = jnp.tile(jnp.concatenate([zero64, sa32, zero32], axis=1), (1, MLA_HEADS))
    sbq = jnp.tile(jnp.concatenate([zero64, sb32, zero32], axis=1), (1, MLA_HEADS))
    tabs = (cq, saq, sbq, jnp.tile(c32, (1, 8)), jnp.tile(sa32, (1, 8)), jnp.tile(sb32, (1, 8)),
            jnp.tile(c64, (1, 4)), jnp.tile(sa64, (1, 4)), jnp.tile(sb64, (1, 4)))
    seg = jnp.asarray(np.kron(np.eye(256 // GQA_HEAD_DIM), np.ones((GQA_HEAD_DIM, GQA_HEAD_DIM))), _BF16)

    pair = np.arange(PEER_PAIRS)[:, None]
    spread_even = jnp.asarray(np.arange(2 * PEER_PAIRS)[None, :] == 2 * pair, _BF16)
    spread_odd = jnp.asarray(np.arange(2 * PEER_PAIRS)[None, :] == 2 * pair + 1, _BF16)

    n_rows = seq // GRID_W
    nt = n_lat + n_ctx
    geom = (bsz, seq, clen, n_lat)
    route_block = ROUTE_BLOCK if (seq % ROUTE_BLOCK == 0 and nt % ROUTE_BLOCK == 0) else TOK_BLOCK
    xa = jnp.concatenate([x.reshape(n_lat, d), ctx.reshape(n_ctx, d)], axis=0)

    for l in range(depth):
        mod_l = mod[l].reshape(MOD_ROWS, 1, N_MOD * d)
        g_mix = norm_mix[l].reshape(1, d)
        mq, mk, mvt, na, navt, df, dvt, gqa, gvt = _proj_prep(
            xa, mod_l, g_mix, _reorder_w_in(w_in[l]).astype(_BF16),
            mla_q_norm[l].reshape(1, -1), mla_kv_norm[l].reshape(1, -1),
            _pad_heads(mla_w_uq[l], MLA_HEADS, MLA_QK).astype(_BF16), mla_w_ukv[l].astype(_BF16),
            jnp.tile(gqa_q_norm[l], GQA_Q_HEADS).reshape(1, -1), jnp.tile(gqa_k_norm[l], 4).reshape(1, -1),
            seg, tabs, n_lat_blocks, blocks_per_batch, bsz, blocks_per_batch)

        def attend(src, k_src, vt_all, **kw):
            return jnp.concatenate([_attention(src, k_src, vt_all, geom, tq, True, **kw),
                                    _attention(src, k_src, vt_all, geom, clen, False, **kw)], axis=0)

        o_a = attend(mq, mk, mvt, q_col=0, k_col=0, kv_pairs=2, padded_heads=True)
        o_b = jnp.concatenate([
            _na_attention(na, geom, _na_bias_table(na_rpb[l], n_rows)),
            _attention(na, na, navt, geom, clen, False, q_col=0, k_col=2, kv_pairs=2)], axis=0)
        lam_vecs = jnp.stack([diff_lam_q1[l], diff_lam_k1[l], diff_lam_q2[l], diff_lam_k2[l]], axis=0)
        lam_init = 0.8 - 0.6 * math.exp(-0.3 * l)
        o_c = attend(df, df, dvt, q_col=0, k_col=2, kv_pairs=2,
                     diff=(lam_vecs, diff_subln[l].reshape(-1, 1), lam_init))
        o_d = attend(gqa, gqa, gvt, q_col=0, k_col=2, kv_pairs=1)

        w_br = w_branch[l].astype(_BF16)
        w_br_d = w_br[3].reshape(GQA_Q_HEADS, GQA_HEAD_DIM, d)[jnp.array(GQA_Q_ORDER)].reshape(-1, d)
        w_br = jnp.concatenate([w_br[:3], w_br_d[None]], axis=0)
        xa = _merge(xa, mod_l, g_mix, (o_a, o_b, o_c, o_d), w_gate[l].astype(_BF16), b_gate[l].reshape(4, 1, d),
                    w_br, w_out[l].astype(_BF16), blocks_per_batch, bsz)

        wq_hi, wq_lo = _split_bf16(peer_w_q[l])
        sk = peer_subkeys[l].reshape(2 * PEER_HEADS, PEER_NKEYS, PEER_QDIM // 2)
        sk_hi, sk_lo = _split_bf16(sk)
        h2, experts, gates = _peer_route(xa, mod_l, norm_ffn[l].reshape(1, d), wq_hi, wq_lo, sk_hi, sk_lo,
                                         route_block, seq, bsz)
        idx = (experts.reshape(PEER_PAIRS, nt).T * PACK_ROWS).reshape(-1)
        gate3 = gates.reshape(PEER_PAIRS, nt // PEER_TB, PEER_TB).transpose(1, 0, 2)
        w3 = _peer_u(idx, h2, gate3, _pack_table(peer_u[l]))
        xa = _peer_v(idx, w3, xa, mod_l, spread_even, spread_odd, _pack_table(peer_v[l]), seq, bsz)

    return _final_norm(xa[:n_lat], final_norm.reshape(1, d)).reshape(bsz, seq, d)
```

```python
import functools
import math

import numpy as np
import jax
import jax.numpy as jnp
from jax import lax
from jax.experimental import pallas as pl
from jax.experimental.pallas import tpu as pltpu

D_MODEL = 1024
GRID_W = 64
EPS = 1e-6
ROPE_BASE = 10000.0
N_MOD = 6

MLA_HEADS = 4
MLA_Q_RANK = 256
MLA_KV_RANK = 128
MLA_NOPE = 64
MLA_ROPE = 32
MLA_V = 64
MLA_QK = MLA_NOPE + MLA_ROPE

NA_HEADS = 4
NA_HEAD_DIM = 64
NA_WIN_ROWS = 8
NA_WIN_COLS = 16

DIFF_HEADS = 4
DIFF_HALF = 32
DIFF_V = 2 * DIFF_HALF

GQA_Q_HEADS = 4
GQA_KV_HEADS = 2
GQA_HEAD_DIM = 64

PEER_HEADS = 8
PEER_NKEYS = 128
PEER_EXPERTS = PEER_NKEYS * PEER_NKEYS
PEER_QDIM = 256
PEER_TOPK = 16
PEER_PAIRS = PEER_HEADS * PEER_TOPK

LANES = 128
SUBLANES = 8
VMEM_LIMIT_BYTES = 56 * 1024 * 1024

COL_QC = 0
COL_KVC = 256
COL_NA = 384
COL_DIFF = 1152
COL_GQA = 1920
COL_KR = 2432
IN_COLS_PAD = 2560
MOD_ROWS = 16

TOK_BLOCK = 256
ROUTE_BLOCK = 1024
QUERY_BLOCK = 1024
HEAD_V = 64
NA_ROWS_PER_STEP = 8
PEER_TB = 32
PACK_ROWS = 4
TOK_ROWS = PEER_PAIRS * PACK_ROWS
U_GROUP = 4
GQA_Q_ORDER = (0, 2, 1, 3)

_NT = (((1,), (1,)), ((), ()))
_F32 = jnp.float32
_BF16 = jnp.bfloat16


def _cparams(sem):
    return pltpu.CompilerParams(dimension_semantics=sem, vmem_limit_bytes=VMEM_LIMIT_BYTES)


def _split_bf16(a):
    hi = a.astype(_BF16)
    lo = (a - hi.astype(_F32)).astype(_BF16)
    return hi, lo


def _dot(a, b):
    return jnp.dot(a, b, preferred_element_type=_F32)


def _norm_mod(x, g, shift, scale):
    y = x * lax.rsqrt(jnp.mean(x * x, axis=-1, keepdims=True) + EPS)
    return (y * g) * (1.0 + scale) + shift


def _mod_kernel(c_ref, w_ref, b_ref, o_ref):
    c = c_ref[...]
    s = c * (1.0 / (1.0 + jnp.exp(-c)))
    s_hi, s_lo = _split_bf16(s)
    w_hi, w_lo = _split_bf16(w_ref[0])
    o_ref[0] = _dot(s_hi, w_hi) + _dot(s_hi, w_lo) + _dot(s_lo, w_hi) + b_ref[0]


def _mod_all(cond, w_mod, b_mod):
    depth, d, n = w_mod.shape
    tn = 1536
    return pl.pallas_call(
        _mod_kernel,
        grid=(depth, n // tn),
        in_specs=[
            pl.BlockSpec((MOD_ROWS, d), lambda l, j: (0, 0)),
            pl.BlockSpec((1, d, tn), lambda l, j: (l, 0, j)),
            pl.BlockSpec((1, 1, tn), lambda l, j: (l, 0, j)),
        ],
        out_specs=pl.BlockSpec((1, MOD_ROWS, tn), lambda l, j: (l, 0, j)),
        out_shape=jax.ShapeDtypeStruct((depth, MOD_ROWS, n), _F32),
        compiler_params=_cparams(("arbitrary", "arbitrary")),
        name="adaln_mod",
    )(cond, w_mod, b_mod.reshape(depth, 1, n))


def _rope(t, cos, sa, sb, dist):
    w = t.shape[-1]
    return t * cos + pltpu.roll(t, w - dist, 1) * sa + pltpu.roll(t, dist, 1) * sb


def _seg_mean_sq(t, seg_ref, seg_len):
    hi, lo = _split_bf16(t * t)
    seg = seg_ref[...]
    return (_dot(hi, seg) + _dot(lo, seg)) * (1.0 / seg_len)


def _store_values_t(vt_ref, first_head, block, upper_half_only):
    t = block.T
    tb = t.shape[1]
    ones_pad = jnp.where(lax.broadcasted_iota(jnp.int32, (LANES - HEAD_V, tb), 0) == 0, 1.0, 0.0).astype(_BF16)
    halves = [t[HEAD_V:, :]] if upper_half_only else [t[:HEAD_V, :], t[HEAD_V:, :]]
    for i, half in enumerate(halves):
        vt_ref[first_head + i, 0:HEAD_V, :] = half.astype(_BF16)
        vt_ref[first_head + i, HEAD_V:LANES, :] = ones_pad


def _proj_kernel(x_ref, sh_ref, sc_ref, g_ref, win_ref, qn_ref, kvn_ref, wuq_ref, wukv_ref,
                 gq_ref, gk_ref, seg_ref, cq_ref, saq_ref, sbq_ref, c32_ref, sa32_ref, sb32_ref,
                 c64_ref, sa64_ref, sb64_ref,
                 mq_ref, mk_ref, mvt_ref, na_ref, navt_ref, df_ref, dvt_ref, gqa_ref, gvt_ref):
    h = _norm_mod(x_ref[...], g_ref[...], sh_ref[0], sc_ref[0])
    p = _dot(h.astype(_BF16), win_ref[...])

    qc = p[:, COL_QC:COL_QC + MLA_Q_RANK]
    qn = qc * lax.rsqrt(jnp.mean(qc * qc, axis=-1, keepdims=True) + EPS) * qn_ref[...]
    q = _dot(qn.astype(_BF16), wuq_ref[...])
    mq_ref[...] = (_rope(q, cq_ref[...], saq_ref[...], sbq_ref[...], MLA_ROPE // 4) * MLA_QK ** -0.5).astype(_BF16)
    kvc = p[:, COL_KVC:COL_KVC + MLA_KV_RANK]
    kvn = kvc * lax.rsqrt(jnp.mean(kvc * kvc, axis=-1, keepdims=True) + EPS) * kvn_ref[...]
    kv = _dot(kvn.astype(_BF16), wukv_ref[...])
    kr = _rope(p[:, COL_KR:COL_KR + LANES], c32_ref[:, :LANES], sa32_ref[:, :LANES], sb32_ref[:, :LANES],
               MLA_ROPE // 4)
    lane = lax.broadcasted_iota(jnp.int32, kv.shape, 1)
    mk_ref[...] = jnp.where(lane % LANES < MLA_NOPE, kv, jnp.concatenate([kr] * MLA_HEADS, axis=1)).astype(_BF16)
    for hd in range(MLA_HEADS):
        _store_values_t(mvt_ref, hd, kv[:, LANES * hd:LANES * (hd + 1)], True)

    na_ref[:, 0:256] = (p[:, COL_NA:COL_NA + 256] * NA_HEAD_DIM ** -0.5).astype(_BF16)
    na_ref[:, 256:768] = p[:, COL_NA + 256:COL_NA + 768].astype(_BF16)
    for pr in range(NA_HEADS // 2):
        _store_values_t(navt_ref, 2 * pr, p[:, COL_NA + 512 + LANES * pr:COL_NA + 512 + LANES * (pr + 1)], False)

    dq = p[:, COL_DIFF:COL_DIFF + 256]
    dk = p[:, COL_DIFF + 256:COL_DIFF + 512]
    c32, sa32, sb32 = c32_ref[...], sa32_ref[...], sb32_ref[...]
    df_ref[:, 0:256] = (_rope(dq, c32, sa32, sb32, DIFF_HALF // 4) * DIFF_HALF ** -0.5).astype(_BF16)
    df_ref[:, 256:512] = _rope(dk, c32, sa32, sb32, DIFF_HALF // 4).astype(_BF16)
    for pr in range(DIFF_HEADS // 2):
        _store_values_t(dvt_ref, 2 * pr, p[:, COL_DIFF + 512 + LANES * pr:COL_DIFF + 512 + LANES * (pr + 1)], False)

    gq = p[:, COL_GQA:COL_GQA + 256]
    gk = p[:, COL_GQA + 256:COL_GQA + 384]
    gqn = gq * lax.rsqrt(_seg_mean_sq(gq, seg_ref, GQA_HEAD_DIM) + EPS) * gq_ref[...]
    gk2 = jnp.concatenate([gk, gk], axis=-1)
    gkn = gk2 * lax.rsqrt(_seg_mean_sq(gk2, seg_ref, GQA_HEAD_DIM) + EPS) * gk_ref[...]
    c64, sa64, sb64 = c64_ref[...], sa64_ref[...], sb64_ref[...]
    gqa_ref[:, 0:256] = (_rope(gqn, c64, sa64, sb64, GQA_HEAD_DIM // 4) * GQA_HEAD_DIM ** -0.5).astype(_BF16)
    gqa_ref[:, 256:384] = _rope(gkn, c64, sa64, sb64, GQA_HEAD_DIM // 4)[:, :128].astype(_BF16)
    _store_values_t(gvt_ref, 0, p[:, COL_GQA + 384:COL_GQA + 512], False)


def _proj_prep(xa, mod_l, g_norm, win, qn, kvn, wuq, wukv, gq, gk, seg, tabs, n_lat_blocks, blocks_per_batch,
               n_batch, pos_blocks):
    nt, d = xa.shape
    tb = TOK_BLOCK

    def mrow(i):
        return jnp.minimum(i // blocks_per_batch, n_batch)

    def trow(i):
        return jnp.where(i < n_lat_blocks, i % pos_blocks, pos_blocks)

    full = lambda a: pl.BlockSpec(a.shape, lambda i: (0,) * a.ndim)
    tab_specs = [pl.BlockSpec((tb, t.shape[1]), lambda i: (trow(i), 0)) for t in tabs]
    rows = lambda w: (pl.BlockSpec((tb, w), lambda i: (i, 0)), jax.ShapeDtypeStruct((nt, w), _BF16))
    cols = lambda nh: (pl.BlockSpec((nh, LANES, tb), lambda i: (0, 0, i)), jax.ShapeDtypeStruct((nh, LANES, nt), _BF16))
    outs = [rows(MLA_HEADS * LANES), rows(MLA_HEADS * LANES), cols(MLA_HEADS), rows(768), cols(NA_HEADS),
            rows(512), cols(DIFF_HEADS), rows(384), cols(GQA_KV_HEADS)]
    return pl.pallas_call(
        _proj_kernel,
        grid=(nt // tb,),
        in_specs=[
            pl.BlockSpec((tb, d), lambda i: (i, 0)),
            pl.BlockSpec((1, 1, d), lambda i: (mrow(i), 0, 0)),
            pl.BlockSpec((1, 1, d), lambda i: (mrow(i), 0, 1)),
            full(g_norm), full(win), full(qn), full(kvn), full(wuq), full(wukv), full(gq), full(gk), full(seg),
        ] + tab_specs,
        out_specs=[o[0] for o in outs],
        out_shape=[o[1] for o in outs],
        compiler_params=_cparams(("arbitrary",)),
        name="mix_in_proj",
    )(xa, mod_l, mod_l, g_norm, win, qn, kvn, wuq, wukv, gq, gk, seg, *tabs)


def _softmax_pv(q, k, vt, kc, vct):
    sc = lax.dot_general(kc, q, _NT, preferred_element_type=_F32)
    m = jnp.max(sc, axis=0, keepdims=True)
    if k is not None:
        s = lax.dot_general(k, q, _NT, preferred_element_type=_F32)
        m = jnp.maximum(m, jnp.max(s, axis=0, keepdims=True))
    acc = _dot(vct, jnp.exp((sc - m).astype(_BF16)))
    if k is not None:
        acc = acc + _dot(vt, jnp.exp((s - m).astype(_BF16)))
    return acc[:HEAD_V, :] / acc[HEAD_V:HEAD_V + 1, :]


def _head_operands(q_ref, k_ref, kc_ref, half, padded_heads, use_latent):
    if padded_heads:
        cols = slice(LANES * half, LANES * (half + 1))
        return q_ref[:, cols], (k_ref[:, cols] if use_latent else None), kc_ref[:, cols]
    q = q_ref[...]
    mine = (lax.broadcasted_iota(jnp.int32, q.shape, 1) // HEAD_V) == half
    return jnp.where(mine, q, jnp.zeros_like(q)), (k_ref[...] if use_latent else None), kc_ref[...]


def _attn_kernel(q_ref, k_ref, vt_ref, kc_ref, vct_ref, o_ref, *, use_latent, padded_heads):
    outs = []
    for half in range(2):
        q, k, kc = _head_operands(q_ref, k_ref, kc_ref, half, padded_heads, use_latent)
        outs.append(_softmax_pv(q, k, vt_ref[half], kc, vct_ref[half]))
    o_ref[...] = jnp.concatenate(outs, axis=0).T.astype(o_ref.dtype)


def _diff_kernel(q_ref, k_ref, vt_ref, kc_ref, vct_ref, lam_ref, sub_ref, o_ref, *, use_latent, lam_init):
    lam = (jnp.exp(jnp.sum(lam_ref[0:1, :] * lam_ref[1:2, :], axis=-1, keepdims=True))
           - jnp.exp(jnp.sum(lam_ref[2:3, :] * lam_ref[3:4, :], axis=-1, keepdims=True)) + lam_init)
    q = q_ref[...]
    k = k_ref[...] if use_latent else None
    kc = kc_ref[...]
    part = lax.broadcasted_iota(jnp.int32, q.shape, 1) // DIFF_HALF
    zero = jnp.zeros_like(q)
    outs = []
    for half in range(2):
        o1 = _softmax_pv(jnp.where(part == 2 * half, q, zero), k, vt_ref[half], kc, vct_ref[half])
        o2 = _softmax_pv(jnp.where(part == 2 * half + 1, q, zero), k, vt_ref[half], kc, vct_ref[half])
        o = o1 - lam * o2
        y = o * lax.rsqrt(jnp.mean(o * o, axis=0, keepdims=True) + EPS) * sub_ref[...]
        outs.append(y * (1.0 - lam_init))
    o_ref[...] = jnp.concatenate(outs, axis=0).T.astype(o_ref.dtype)


def _attention(src, k_src, vt_all, geom, tq, use_latent, *, q_col, k_col, kv_pairs, padded_heads=False,
               diff=None):
    bsz, seq, clen, n_lat = geom
    n_pairs = 2
    w = 2 * LANES if padded_heads else LANES
    lq = seq if use_latent else clen
    ctx0 = n_lat // clen
    q0 = 0 if use_latent else n_lat // tq
    kp = (lambda p: p) if kv_pairs > 1 else (lambda p: 0)
    in_specs = [
        pl.BlockSpec((tq, w), lambda b, p, i: (q0 + b * (lq // tq) + i, q_col + p)),
        pl.BlockSpec((seq, w), lambda b, p, i: (b, k_col + kp(p))),
        pl.BlockSpec((2, LANES, seq), lambda b, p, i: (kp(p), 0, b)),
        pl.BlockSpec((clen, w), lambda b, p, i: (ctx0 + b, k_col + kp(p))),
        pl.BlockSpec((2, LANES, clen), lambda b, p, i: (kp(p), 0, ctx0 + b)),
    ]
    args = [src, k_src, vt_all, k_src, vt_all]
    if diff:
        lam_vecs, sub_col, lam_init = diff
        body = functools.partial(_diff_kernel, use_latent=use_latent, lam_init=lam_init)
        in_specs += [pl.BlockSpec(lam_vecs.shape, lambda b, p, i: (0, 0)),
                     pl.BlockSpec(sub_col.shape, lambda b, p, i: (0, 0))]
        args += [lam_vecs, sub_col]
    else:
        body = functools.partial(_attn_kernel, use_latent=use_latent, padded_heads=padded_heads)
    return pl.pallas_call(
        body,
        grid=(bsz, n_pairs, lq // tq),
        in_specs=in_specs,
        out_specs=pl.BlockSpec((tq, LANES), lambda b, p, i: (b * (lq // tq) + i, p)),
        out_shape=jax.ShapeDtypeStruct((bsz * lq, n_pairs * LANES), _BF16),
        compiler_params=_cparams(("arbitrary", "arbitrary", "arbitrary")),
        name="diff_attention" if diff else "softmax_attention",
    )(*args)


def _na_kernel(q_ref, k_ref, v_ref, kc_ref, vc_ref, bias_ref, o_ref, *, n_rows, kr, rows_per_step):
    kc, vc = kc_ref[...], vc_ref[...]
    for i in range(rows_per_step):
        r = pl.program_id(2) * rows_per_step + i
        rs = jnp.clip(r - kr // 2, 0, n_rows - kr)
        start = pl.multiple_of(rs * GRID_W, GRID_W)
        q = q_ref[i * GRID_W:(i + 1) * GRID_W, :]
        k_win = k_ref[pl.ds(start, kr * GRID_W), :]
        v_win = v_ref[pl.ds(start, kr * GRID_W), :]
        lane = lax.broadcasted_iota(jnp.int32, q.shape, 1)
        outs = []
        for half in range(2):
            qh = jnp.where(lane // HEAD_V == half, q, jnp.zeros_like(q))
            s_loc = lax.dot_general(qh, k_win, _NT, preferred_element_type=_F32) + bias_ref[half, r - rs]
            s_ctx = lax.dot_general(qh, kc, _NT, preferred_element_type=_F32)
            m = jnp.maximum(jnp.max(s_loc, axis=-1, keepdims=True), jnp.max(s_ctx, axis=-1, keepdims=True))
            e_loc = jnp.exp((s_loc - m).astype(_BF16))
            e_ctx = jnp.exp((s_ctx - m).astype(_BF16))
            l = (jnp.sum(e_loc.astype(_F32), axis=-1, keepdims=True)
                 + jnp.sum(e_ctx.astype(_F32), axis=-1, keepdims=True))
            outs.append((_dot(e_loc, v_win) + _dot(e_ctx, vc)) / l)
        o_ref[i * GRID_W:(i + 1) * GRID_W, :] = jnp.where(lane // HEAD_V == 0, outs[0], outs[1]).astype(o_ref.dtype)


def _na_attention(src, geom, bias):
    bsz, seq, clen, n_lat = geom
    n_rows = seq // GRID_W
    kr = min(NA_WIN_ROWS, n_rows)
    rows_per_step = math.gcd(n_rows, NA_ROWS_PER_STEP)
    tq = rows_per_step * GRID_W
    ctx0 = n_lat // clen
    n_pairs = NA_HEADS // 2
    return pl.pallas_call(
        functools.partial(_na_kernel, n_rows=n_rows, kr=kr, rows_per_step=rows_per_step),
        grid=(bsz, n_pairs, n_rows // rows_per_step),
        in_specs=[
            pl.BlockSpec((tq, LANES), lambda b, p, r: (b * (seq // tq) + r, p)),
            pl.BlockSpec((seq, LANES), lambda b, p, r: (b, n_pairs + p)),
            pl.BlockSpec((seq, LANES), lambda b, p, r: (b, 2 * n_pairs + p)),
            pl.BlockSpec((clen, LANES), lambda b, p, r: (ctx0 + b, n_pairs + p)),
            pl.BlockSpec((clen, LANES), lambda b, p, r: (ctx0 + b, 2 * n_pairs + p)),
            pl.BlockSpec((2,) + bias.shape[1:], lambda b, p, r: (p, 0, 0, 0)),
        ],
        out_specs=pl.BlockSpec((tq, LANES), lambda b, p, r: (b * (seq // tq) + r, p)),
        out_shape=jax.ShapeDtypeStruct((n_lat, n_pairs * LANES), _BF16),
        compiler_params=_cparams(("arbitrary", "arbitrary", "arbitrary")),
        name="neighbourhood_attention",
    )(src, src, src, src, src, bias)


def _na_bias_table(rpb, n_rows):
    kr = min(NA_WIN_ROWS, n_rows)
    nc = NA_WIN_COLS
    col_q = np.arange(GRID_W)
    col_start = np.clip(col_q - nc // 2, 0, GRID_W - nc)
    key_c = np.arange(GRID_W)
    inside = (key_c[None, :] >= col_start[:, None]) & (key_c[None, :] < col_start[:, None] + nc)
    col_off = key_c[None, :] - col_q[:, None] + (NA_WIN_COLS - 1)
    col_sel = (inside[:, :, None] & (col_off[:, :, None] == np.arange(2 * NA_WIN_COLS - 1))).astype(np.float32)
    row_off = np.arange(kr)[None, :] - np.arange(kr)[:, None] + (NA_WIN_ROWS - 1)
    row_sel = (row_off[:, :, None] == np.arange(2 * NA_WIN_ROWS - 1)).astype(np.float32)
    tab = jnp.einsum('hrj,dir,cxj->hdcix', rpb, row_sel, col_sel, precision=lax.Precision.HIGHEST)
    tab = jnp.where(inside[None, None, :, None, :], tab, -1e30)
    return tab.reshape(rpb.shape[0], kr, GRID_W, kr * GRID_W).astype(_F32)


def _merge_kernel(x_ref, sh_ref, sc_ref, g1_ref, g_ref, oa_ref, ob_ref, oc_ref, od_ref, wg_ref, bg_ref, wb_ref, wo_ref,
                  y_ref):
    x = x_ref[...]
    hb = _norm_mod(x, g_ref[...], sh_ref[0], sc_ref[0]).astype(_BF16)
    merged = None
    for i, o_ref in enumerate((oa_ref, ob_ref, oc_ref, od_ref)):
        gate = 1.0 / (1.0 + jnp.exp(-(_dot(hb, wg_ref[i]) + bg_ref[i])))
        term = gate * _dot(o_ref[...], wb_ref[i])
        merged = term if merged is None else merged + term
    y_ref[...] = x + g1_ref[0] * _dot(merged.astype(_BF16), wo_ref[...])


def _merge(xa, mod_l, g_norm, outs, wg, bg, wb, wo, blocks_per_batch, n_batch):
    nt, d = xa.shape
    tb = TOK_BLOCK

    def mrow(i):
        return jnp.minimum(i // blocks_per_batch, n_batch)

    full = lambda a: pl.BlockSpec(a.shape, lambda i: (0,) * a.ndim)
    return pl.pallas_call(
        _merge_kernel,
        grid=(nt // tb,),
        in_specs=[
            pl.BlockSpec((tb, d), lambda i: (i, 0)),
            pl.BlockSpec((1, 1, d), lambda i: (mrow(i), 0, 0)),
            pl.BlockSpec((1, 1, d), lambda i: (mrow(i), 0, 1)),
            pl.BlockSpec((1, 1, d), lambda i: (mrow(i), 0, 2)),
            full(g_norm),
        ] + [pl.BlockSpec((tb, o.shape[1]), lambda i: (i, 0)) for o in outs] + [
            full(wg), full(bg), full(wb), full(wo),
        ],
        out_specs=pl.BlockSpec((tb, d), lambda i: (i, 0)),
        out_shape=jax.ShapeDtypeStruct((nt, d), _F32),
        compiler_params=_cparams(("arbitrary",)),
        name="branch_merge",
    )(xa, mod_l, mod_l, mod_l, g_norm, *outs, wg, bg, wb, wo)


def _top_rows(s, payload, k):
    n = s.shape[0]
    iota = lax.broadcasted_iota(jnp.int32, s.shape, 0).astype(_F32)
    vals, pays = [], []
    for _ in range(k):
        m = jnp.max(s, axis=0, keepdims=True)
        pos = jnp.min(jnp.where(s == m, iota, float(n)), axis=0, keepdims=True)
        hit = iota == pos
        vals.append(m)
        if payload is None:
            pays.append(pos)
        else:
            pays.append(jnp.max(jnp.where(hit, payload, -1.0), axis=0, keepdims=True))
        s = jnp.where(hit, -jnp.inf, s)
    return jnp.concatenate(vals, axis=0), jnp.concatenate(pays, axis=0)


def _candidate_grid(s1, s2, combine):
    k = PEER_TOPK
    blocks = [combine(s1[0:1, :], s2)]
    blocks += [combine(s1[a:a + 1, :], s2[0:k // 2, :]) for a in range(1, k // 2)]
    blocks.append(combine(s1[k // 2:k, :], s2[0:1, :]))
    return jnp.concatenate(blocks, axis=0)


def _route_kernel(x_ref, sh_ref, sc_ref, g_ref, wqh_ref, wql_ref, skh_ref, skl_ref,
                  h_ref, e_ref, gate_ref, q_s, top_s, idx_s):
    h = _norm_mod(x_ref[...], g_ref[...], sh_ref[0], sc_ref[0])
    h_ref[...] = h.astype(_BF16)
    h_hi, h_lo = _split_bf16(h)
    q = _dot(h_hi, wqh_ref[...]) + _dot(h_hi, wql_ref[...]) + _dot(h_lo, wqh_ref[...])
    for hp in range(2 * PEER_HEADS):
        q_s[hp] = q[:, LANES * hp:LANES * (hp + 1)]

    def sub_scores(hp, c):
        q_hi, q_lo = _split_bf16(q_s[hp])
        sk_hi, sk_lo = skh_ref[hp], skl_ref[hp]
        s = (lax.dot_general(sk_hi, q_hi, _NT, preferred_element_type=_F32)
             + lax.dot_general(sk_hi, q_lo, _NT, preferred_element_type=_F32)
             + lax.dot_general(sk_lo, q_hi, _NT, preferred_element_type=_F32))
        vals, idx = _top_rows(s, None, PEER_TOPK)
        top_s[hp] = vals
        idx_s[hp] = idx
        return c

    lax.fori_loop(0, 2 * PEER_HEADS, sub_scores, 0)

    def head(hd, c):
        cand = _candidate_grid(top_s[2 * hd], top_s[2 * hd + 1], lambda a, b: a + b)
        cidx = _candidate_grid(idx_s[2 * hd], idx_s[2 * hd + 1], lambda a, b: a * float(PEER_NKEYS) + b)
        best, experts = _top_rows(cand, cidx, PEER_TOPK)
        ex = jnp.exp(best - best[0:1, :])
        gate_ref[hd] = ex / jnp.sum(ex, axis=0, keepdims=True)
        e_ref[hd] = experts.astype(jnp.int32)
        return c

    lax.fori_loop(0, PEER_HEADS, head, 0)


def _peer_route(xa, mod_l, g_norm, wq_hi, wq_lo, sk_hi, sk_lo, tb, seq, n_batch):
    nt, d = xa.shape

    def mrow(i):
        return jnp.minimum(i * tb // seq, n_batch)

    full = lambda a: pl.BlockSpec(a.shape, lambda i: (0,) * a.ndim)
    return pl.pallas_call(
        _route_kernel,
        grid=(nt // tb,),
        in_specs=[
            pl.BlockSpec((tb, d), lambda i: (i, 0)),
            pl.BlockSpec((1, 1, d), lambda i: (mrow(i), 0, 3)),
            pl.BlockSpec((1, 1, d), lambda i: (mrow(i), 0, 4)),
            full(g_norm), full(wq_hi), full(wq_lo), full(sk_hi), full(sk_lo),
        ],
        out_specs=[
            pl.BlockSpec((tb, d), lambda i: (i, 0)),
            pl.BlockSpec((PEER_HEADS, PEER_TOPK, tb), lambda i: (0, 0, i)),
            pl.BlockSpec((PEER_HEADS, PEER_TOPK, tb), lambda i: (0, 0, i)),
        ],
        out_shape=[
            jax.ShapeDtypeStruct((nt, d), _BF16),
            jax.ShapeDtypeStruct((PEER_HEADS, PEER_TOPK, nt), jnp.int32),
            jax.ShapeDtypeStruct((PEER_HEADS, PEER_TOPK, nt), _F32),
        ],
        scratch_shapes=[
            pltpu.VMEM((2 * PEER_HEADS, tb, LANES), _F32),
            pltpu.VMEM((2 * PEER_HEADS, PEER_TOPK, tb), _F32),
            pltpu.VMEM((2 * PEER_HEADS, PEER_TOPK, tb), _F32),
        ],
        compiler_params=_cparams(("arbitrary",)),
        name="peer_route",
    )(xa, mod_l, mod_l, g_norm, wq_hi, wq_lo, sk_hi, sk_lo)


def _gather_token(idx_ref, tab_ref, tile_ref, t):
    dst = t * TOK_ROWS
    if not isinstance(t, int):
        dst = pl.multiple_of(dst, TOK_ROWS)
    for p in range(PEER_PAIRS):
        row = pl.multiple_of(idx_ref[t * PEER_PAIRS + p], PACK_ROWS)
        tile_ref[pl.ds(dst + PACK_ROWS * p, PACK_ROWS), :] = tab_ref[pl.ds(row, PACK_ROWS), :]


def _first_block_gather(idx_ref, tab_ref, tile_ref, tb):
    @pl.when(pl.program_id(0) == 0)
    def _():
        def body(t, c):
            _gather_token(idx_ref, tab_ref, tile_ref, t)
            return c

        lax.fori_loop(0, tb, body, 0)


def _ping_pong(fn, tile_a, tile_b):
    parity = pl.program_id(0) % 2

    @pl.when(parity == 0)
    def _():
        fn(tile_a, tile_b)

    @pl.when(parity == 1)
    def _():
        fn(tile_b, tile_a)


def _unpack_f32(w):
    lo = pltpu.bitcast(w << 16, _F32)
    hi = pltpu.bitcast(w & jnp.uint32(0xFFFF0000), _F32)
    return lo, hi


def _peer_u_kernel(idx_ref, idx_next_ref, h_ref, gate_ref, tab_ref, w_ref, tile_a, tile_b):
    tb = h_ref.shape[0]
    _first_block_gather(idx_ref, tab_ref, tile_a, tb)
    lane = lax.broadcasted_iota(jnp.int32, (PEER_PAIRS, tb), 1)

    def run(src, dst):
        h = h_ref[...]
        a = jnp.zeros((PEER_PAIRS, tb), _F32)
        for g in range(tb // U_GROUP):
            t0 = g * U_GROUP
            los, his = [], []
            for j in range(PACK_ROWS):
                lo, hi = _unpack_f32(src[pl.ds(t0 * TOK_ROWS + j, U_GROUP * PEER_PAIRS, stride=PACK_ROWS), :])
                los.append(lo.astype(_BF16))
                his.append(hi.astype(_BF16))
            rows = jnp.concatenate(los + his, axis=1)
            acc = lax.dot_general(rows, h, _NT, preferred_element_type=_F32)
            for k in range(U_GROUP):
                a = a + jnp.where(lane == t0 + k, acc[PEER_PAIRS * k:PEER_PAIRS * (k + 1), :], 0.0)
            for k in range(U_GROUP):
                _gather_token(idx_next_ref, tab_ref, dst, t0 + k)
        act = 0.5 * a * (1.0 + lax.erf(a * (2.0 ** -0.5)))
        w_ref[0] = gate_ref[0] * act

    _ping_pong(run, tile_a, tile_b)


def _peer_v_kernel(idx_ref, idx_next_ref, w_ref, x_ref, g2_ref, pe_ref, po_ref, tab_ref, y_ref,
                   tile_a, tile_b, lhs_ref):
    tb = x_ref.shape[0]
    _first_block_gather(idx_ref, tab_ref, tile_a, tb)

    @pl.when(pl.program_id(0) == 0)
    def _():
        lhs_ref[...] = jnp.zeros(lhs_ref.shape, lhs_ref.dtype)

    w_hi, w_lo = _split_bf16(w_ref[0].T)
    v = 0
    for spread_ref in (pe_ref, po_ref):
        for part in (w_hi, w_lo):
            val = _dot(part, spread_ref[...])
            for hf in range(2):
                lhs_ref[hf, pl.ds(v, tb, stride=SUBLANES), :] = val[:, LANES * hf:LANES * (hf + 1)]
            v += 1
    sub = lax.broadcasted_iota(jnp.int32, (SUBLANES, LANES), 0)

    def run(src, dst):
        g2 = g2_ref[0]
        for g in range(tb // SUBLANES):
            grp = [jnp.zeros((SUBLANES, LANES), _F32) for _ in range(2 * PACK_ROWS)]
            for k in range(SUBLANES):
                t = g * SUBLANES + k
                lhs = jnp.concatenate([lhs_ref[0, pl.ds(t * SUBLANES, SUBLANES), :],
                                       lhs_ref[1, pl.ds(t * SUBLANES, SUBLANES), :]], axis=1).astype(_BF16)
                for j in range(PACK_ROWS):
                    rows = pltpu.bitcast(src[pl.ds(t * TOK_ROWS + j, PEER_PAIRS, stride=PACK_ROWS), :], _BF16)
                    res = _dot(lhs, rows)
                    lo = res[0:1, :] + res[1:2, :]
                    hi = res[2:3, :] + res[3:4, :]
                    grp[j] = jnp.where(sub == k, lo, grp[j])
                    grp[PACK_ROWS + j] = jnp.where(sub == k, hi, grp[PACK_ROWS + j])
                _gather_token(idx_next_ref, tab_ref, dst, t)
            r0 = g * SUBLANES
            y_ref[r0:r0 + SUBLANES, :] = x_ref[r0:r0 + SUBLANES, :] + g2 * jnp.concatenate(grp, axis=1)

    _ping_pong(run, tile_a, tile_b)


def _table_spec(tab):
    return pl.BlockSpec(tab.shape, lambda i: (0, 0), pipeline_mode=pl.Buffered(1))


def _pair_specs(n_blocks):
    n_pairs = PEER_TB * PEER_PAIRS
    return [pl.BlockSpec((n_pairs,), lambda i: (i,), memory_space=pltpu.SMEM),
            pl.BlockSpec((n_pairs,), lambda i: (jnp.minimum(i + 1, n_blocks - 1),), memory_space=pltpu.SMEM)]


def _tile_scratch():
    return [pltpu.VMEM((PEER_TB * TOK_ROWS, LANES), jnp.uint32)] * 2


def _peer_u(idx, h, gate3, tab_u):
    nt, d = h.shape
    tb = PEER_TB
    return pl.pallas_call(
        _peer_u_kernel,
        grid=(nt // tb,),
        in_specs=_pair_specs(nt // tb) + [
            pl.BlockSpec((tb, d), lambda i: (i, 0)),
            pl.BlockSpec((1, PEER_PAIRS, tb), lambda i: (i, 0, 0)),
            _table_spec(tab_u),
        ],
        out_specs=pl.BlockSpec((1, PEER_PAIRS, tb), lambda i: (i, 0, 0)),
        out_shape=jax.ShapeDtypeStruct((nt // tb, PEER_PAIRS, tb), _F32),
        scratch_shapes=_tile_scratch(),
        compiler_params=_cparams(("arbitrary",)),
        name="peer_expert_in",
    )(idx, idx, h, gate3, tab_u)


def _peer_v(idx, w3, xa, mod_l, spread_even, spread_odd, tab_v, seq, n_batch):
    nt, d = xa.shape
    tb = PEER_TB

    def mrow(i):
        return jnp.minimum(i * tb // seq, n_batch)

    full = lambda a: pl.BlockSpec(a.shape, lambda i: (0,) * a.ndim)
    return pl.pallas_call(
        _peer_v_kernel,
        grid=(nt // tb,),
        in_specs=_pair_specs(nt // tb) + [
            pl.BlockSpec((1, PEER_PAIRS, tb), lambda i: (i, 0, 0)),
            pl.BlockSpec((tb, d), lambda i: (i, 0)),
            pl.BlockSpec((1, 1, d), lambda i: (mrow(i), 0, 5)),
            full(spread_even), full(spread_odd),
            _table_spec(tab_v),
        ],
        out_specs=pl.BlockSpec((tb, d), lambda i: (i, 0)),
        out_shape=jax.ShapeDtypeStruct((nt, d), _F32),
        scratch_shapes=_tile_scratch() + [pltpu.VMEM((2, tb * SUBLANES, LANES), _F32)],
        compiler_params=_cparams(("arbitrary",)),
        name="peer_expert_out",
    )(idx, idx, w3, xa, mod_l, spread_even, spread_odd, tab_v)


def _final_kernel(x_ref, g_ref, o_ref):
    x = x_ref[...]
    o_ref[...] = x * lax.rsqrt(jnp.mean(x * x, axis=-1, keepdims=True) + EPS) * g_ref[...]


def _final_norm(x, g):
    n, d = x.shape
    tb = TOK_BLOCK
    return pl.pallas_call(
        _final_kernel,
        grid=(n // tb,),
        in_specs=[pl.BlockSpec((tb, d), lambda i: (i, 0)), pl.BlockSpec((1, d), lambda i: (0, 0))],
        out_specs=pl.BlockSpec((tb, d), lambda i: (i, 0)),
        out_shape=jax.ShapeDtypeStruct((n, d), _F32),
        compiler_params=_cparams(("arbitrary",)),
        name="final_norm",
    )(x, g)


def _rope_tables(seq, dr):
    pos = jnp.arange(seq)
    rows = (pos // GRID_W).astype(_F32)
    cols = (pos % GRID_W).astype(_F32)
    h = dr // 2
    inv_freq = jnp.power(ROPE_BASE, -jnp.arange(0, h, 2, dtype=_F32) / h)
    ang_r = rows[:, None] * inv_freq[None, :]
    ang_c = cols[:, None] * inv_freq[None, :]
    cr, sr, cc, sc = jnp.cos(ang_r), jnp.sin(ang_r), jnp.cos(ang_c), jnp.sin(ang_c)
    z = jnp.zeros_like(sr)
    cos = jnp.concatenate([cr, cr, cc, cc], axis=-1)
    sa = jnp.concatenate([-sr, z, -sc, z], axis=-1)
    sb = jnp.concatenate([z, sr, z, sc], axis=-1)
    pad1 = jnp.ones((TOK_BLOCK, dr), _F32)
    pad0 = jnp.zeros((TOK_BLOCK, dr), _F32)
    return (jnp.concatenate([cos, pad1], axis=0), jnp.concatenate([sa, pad0], axis=0),
            jnp.concatenate([sb, pad0], axis=0))


def _pack_table(t):
    e, d = t.shape
    bits = lax.bitcast_convert_type(t.astype(_BF16), jnp.uint16).astype(jnp.uint32)
    packed = bits[:, :d // 2] | (bits[:, d // 2:] << 16)
    return packed.reshape(e * PACK_ROWS, LANES)


def _reorder_w_in(w_in):
    d = w_in.shape[0]
    gq0 = 416 + 768 + 768
    gq = w_in[:, gq0:gq0 + 256].reshape(d, GQA_Q_HEADS, GQA_HEAD_DIM)[:, jnp.array(GQA_Q_ORDER), :].reshape(d, 256)
    zeros = lambda n: jnp.zeros((d, n), w_in.dtype)
    return jnp.concatenate([w_in[:, :384], w_in[:, 416:gq0], gq, w_in[:, gq0 + 256:],
                            zeros(MLA_NOPE), w_in[:, 384:416], zeros(LANES - MLA_NOPE - MLA_ROPE)], axis=1)


def _pad_heads(w, n_heads, width):
    k = w.shape[0]
    w = w.reshape(k, n_heads, width)
    return jnp.concatenate([w, jnp.zeros((k, n_heads, LANES - width), w.dtype)], axis=-1).reshape(k, n_heads * LANES)


def kernel(x, c, ctx, c_ctx, w_mod, b_mod, norm_mix, norm_ffn, w_in, mla_q_norm, mla_kv_norm, mla_w_uq,
           mla_w_ukv, na_rpb, diff_lam_q1, diff_lam_k1, diff_lam_q2, diff_lam_k2, diff_subln, gqa_q_norm,
           gqa_k_norm, w_branch, w_gate, b_gate, w_out, peer_w_q, peer_subkeys, peer_u, peer_v, final_norm):
    bsz, seq, d = x.shape
    clen = ctx.shape[1]
    depth = w_mod.shape[0]
    n_lat, n_ctx = bsz * seq, bsz * clen
    assert d == D_MODEL and seq % TOK_BLOCK == 0 and clen % TOK_BLOCK == 0 and seq % GRID_W == 0
    assert bsz < MOD_ROWS and seq % PEER_TB == 0 and clen % PEER_TB == 0
    blocks_per_batch = seq // TOK_BLOCK
    n_lat_blocks = n_lat // TOK_BLOCK
    tq = math.gcd(QUERY_BLOCK, seq)

    cond = jnp.zeros((MOD_ROWS, d), _F32).at[:bsz].set(c).at[bsz].set(c_ctx)
    mod = _mod_all(cond, w_mod, b_mod)

    c32, sa32, sb32 = _rope_tables(seq, MLA_ROPE)
    c64, sa64, sb64 = _rope_tables(seq, GQA_HEAD_DIM)
    rows_t = seq + TOK_BLOCK
    one64 = jnp.ones((rows_t, MLA_NOPE), _F32)
    zero64 = jnp.zeros((rows_t, MLA_NOPE), _F32)
    one32 = jnp.ones((rows_t, LANES - MLA_QK), _F32)
    zero32 = jnp.zeros((rows_t, LANES - MLA_QK), _F32)
    cq = jnp.tile(jnp.concatenate([one64, c32, one32], axis=1), (1, MLA_HEADS))
    saq = jnp.tile(jnp.concatenate([zero64, sa32, zero32], axis=1), (1, MLA_HEADS))
    sbq = jnp.tile(jnp.concatenate([zero64, sb32, zero32], axis=1), (1, MLA_HEADS))
    tabs = (cq, saq, sbq, jnp.tile(c32, (1, 8)), jnp.tile(sa32, (1, 8)), jnp.tile(sb32, (1, 8)),
            jnp.tile(c64, (1, 4)), jnp.tile(sa64, (1, 4)), jnp.tile(sb64, (1, 4)))
    seg = jnp.asarray(np.kron(np.eye(256 // GQA_HEAD_DIM), np.ones((GQA_HEAD_DIM, GQA_HEAD_DIM))), _BF16)

    pair = np.arange(PEER_PAIRS)[:, None]
    spread_even = jnp.asarray(np.arange(2 * PEER_PAIRS)[None, :] == 2 * pair, _BF16)
    spread_odd = jnp.asarray(np.arange(2 * PEER_PAIRS)[None, :] == 2 * pair + 1, _BF16)

    n_rows = seq // GRID_W
    nt = n_lat + n_ctx
    geom = (bsz, seq, clen, n_lat)
    route_block = ROUTE_BLOCK if (seq % ROUTE_BLOCK == 0 and nt % ROUTE_BLOCK == 0) else TOK_BLOCK
    xa = jnp.concatenate([x.reshape(n_lat, d), ctx.reshape(n_ctx, d)], axis=0)

    for l in range(depth):
        mod_l = mod[l].reshape(MOD_ROWS, 1, N_MOD * d)
        g_mix = norm_mix[l].reshape(1, d)
        mq, mk, mvt, na, navt, df, dvt, gqa, gvt = _proj_prep(
            xa, mod_l, g_mix, _reorder_w_in(w_in[l]).astype(_BF16),
            mla_q_norm[l].reshape(1, -1), mla_kv_norm[l].reshape(1, -1),
            _pad_heads(mla_w_uq[l], MLA_HEADS, MLA_QK).astype(_BF16), mla_w_ukv[l].astype(_BF16),
            jnp.tile(gqa_q_norm[l], GQA_Q_HEADS).reshape(1, -1), jnp.tile(gqa_k_norm[l], 4).reshape(1, -1),
            seg, tabs, n_lat_blocks, blocks_per_batch, bsz, blocks_per_batch)

        def attend(src, k_src, vt_all, **kw):
            return jnp.concatenate([_attention(src, k_src, vt_all, geom, tq, True, **kw),
                                    _attention(src, k_src, vt_all, geom, clen, False, **kw)], axis=0)

        o_a = attend(mq, mk, mvt, q_col=0, k_col=0, kv_pairs=2, padded_heads=True)
        o_b = jnp.concatenate([
            _na_attention(na, geom, _na_bias_table(na_rpb[l], n_rows)),
            _attention(na, na, navt, geom, clen, False, q_col=0, k_col=2, kv_pairs=2)], axis=0)
        lam_vecs = jnp.stack([diff_lam_q1[l], diff_lam_k1[l], diff_lam_q2[l], diff_lam_k2[l]], axis=0)
        lam_init = 0.8 - 0.6 * math.exp(-0.3 * l)
        o_c = attend(df, df, dvt, q_col=0, k_col=2, kv_pairs=2,
                     diff=(lam_vecs, diff_subln[l].reshape(-1, 1), lam_init))
        o_d = attend(gqa, gqa, gvt, q_col=0, k_col=2, kv_pairs=1)

        w_br = w_branch[l].astype(_BF16)
        w_br_d = w_br[3].reshape(GQA_Q_HEADS, GQA_HEAD_DIM, d)[jnp.array(GQA_Q_ORDER)].reshape(-1, d)
        w_br = jnp.concatenate([w_br[:3], w_br_d[None]], axis=0)
        xa = _merge(xa, mod_l, g_mix, (o_a, o_b, o_c, o_d), w_gate[l].astype(_BF16), b_gate[l].reshape(4, 1, d),
                    w_br, w_out[l].astype(_BF16), blocks_per_batch, bsz)

        wq_hi, wq_lo = _split_bf16(peer_w_q[l])
        sk = peer_subkeys[l].reshape(2 * PEER_HEADS, PEER_NKEYS, PEER_QDIM // 2)
        sk_hi, sk_lo = _split_bf16(sk)
        h2, experts, gates = _peer_route(xa, mod_l, norm_ffn[l].reshape(1, d), wq_hi, wq_lo, sk_hi, sk_lo,
                                         route_block, seq, bsz)
        idx = (experts.reshape(PEER_PAIRS, nt).T * PACK_ROWS).reshape(-1)
        gate3 = gates.reshape(PEER_PAIRS, nt // PEER_TB, PEER_TB).transpose(1, 0, 2)
        w3 = _peer_u(idx, h2, gate3, _pack_table(peer_u[l]))
        xa = _peer_v(idx, w3, xa, mod_l, spread_even, spread_odd, _pack_table(peer_v[l]), seq, bsz)

    return _final_norm(xa[:n_lat], final_norm.reshape(1, d)).reshape(bsz, seq, d)
```

```python
import functools
import math

import numpy as np
import jax
import jax.numpy as jnp
from jax import lax
from jax.experimental import pallas as pl
from jax.experimental.pallas import tpu as pltpu

D_MODEL = 1024
GRID_W = 64
EPS = 1e-6
ROPE_BASE = 10000.0
N_MOD = 6

MLA_HEADS = 4
MLA_Q_RANK = 256
MLA_KV_RANK = 128
MLA_NOPE = 64
MLA_ROPE = 32
MLA_V = 64
MLA_QK = MLA_NOPE + MLA_ROPE

NA_HEADS = 4
NA_HEAD_DIM = 64
NA_WIN_ROWS = 8
NA_WIN_COLS = 16

DIFF_HEADS = 4
DIFF_HALF = 32
DIFF_V = 2 * DIFF_HALF

GQA_Q_HEADS = 4
GQA_KV_HEADS = 2
GQA_HEAD_DIM = 64

PEER_HEADS = 8
PEER_NKEYS = 128
PEER_EXPERTS = PEER_NKEYS * PEER_NKEYS
PEER_QDIM = 256
PEER_TOPK = 16
PEER_PAIRS = PEER_HEADS * PEER_TOPK

LANES = 128
SUBLANES = 8
VMEM_LIMIT_BYTES = 56 * 1024 * 1024

COL_QC = 0
COL_KVC = 256
COL_NA = 384
COL_DIFF = 1152
COL_GQA = 1920
COL_KR = 2432
IN_COLS_PAD = 2560
MOD_ROWS = 16

TOK_BLOCK = 256
ROUTE_BLOCK = 1024
QUERY_BLOCK = 1024
HEAD_V = 64
NA_ROWS_PER_STEP = 8
PEER_TB = 32
PACK_ROWS = 4
TOK_ROWS = PEER_PAIRS * PACK_ROWS
U_GROUP = 4
GQA_Q_ORDER = (0, 2, 1, 3)

_NT = (((1,), (1,)), ((), ()))
_F32 = jnp.float32
_BF16 = jnp.bfloat16


def _cparams(sem):
    return pltpu.CompilerParams(dimension_semantics=sem, vmem_limit_bytes=VMEM_LIMIT_BYTES)


def _split_bf16(a):
    hi = a.astype(_BF16)
    lo = (a - hi.astype(_F32)).astype(_BF16)
    return hi, lo


def _dot(a, b):
    return jnp.dot(a, b, preferred_element_type=_F32)


def _norm_mod(x, g, shift, scale):
    y = x * lax.rsqrt(jnp.mean(x * x, axis=-1, keepdims=True) + EPS)
    return (y * g) * (1.0 + scale) + shift


def _mod_kernel(c_ref, w_ref, b_ref, o_ref):
    c = c_ref[...]
    s = c * (1.0 / (1.0 + jnp.exp(-c)))
    s_hi, s_lo = _split_bf16(s)
    w_hi, w_lo = _split_bf16(w_ref[0])
    o_ref[0] = _dot(s_hi, w_hi) + _dot(s_hi, w_lo) + _dot(s_lo, w_hi) + b_ref[0]


def _mod_all(cond, w_mod, b_mod):
    depth, d, n = w_mod.shape
    tn = 1536
    return pl.pallas_call(
        _mod_kernel,
        grid=(depth, n // tn),
        in_specs=[
            pl.BlockSpec((MOD_ROWS, d), lambda l, j: (0, 0)),
            pl.BlockSpec((1, d, tn), lambda l, j: (l, 0, j)),
            pl.BlockSpec((1, 1, tn), lambda l, j: (l, 0, j)),
        ],
        out_specs=pl.BlockSpec((1, MOD_ROWS, tn), lambda l, j: (l, 0, j)),
        out_shape=jax.ShapeDtypeStruct((depth, MOD_ROWS, n), _F32),
        compiler_params=_cparams(("arbitrary", "arbitrary")),
        name="adaln_mod",
    )(cond, w_mod, b_mod.reshape(depth, 1, n))


def _rope(t, cos, sa, sb, dist):
    w = t.shape[-1]
    return t * cos + pltpu.roll(t, w - dist, 1) * sa + pltpu.roll(t, dist, 1) * sb


def _seg_mean_sq(t, seg_ref, seg_len):
    hi, lo = _split_bf16(t * t)
    seg = seg_ref[...]
    return (_dot(hi, seg) + _dot(lo, seg)) * (1.0 / seg_len)


def _store_values_t(vt_ref, first_head, block, upper_half_only):
    t = block.T
    tb = t.shape[1]
    ones_pad = jnp.where(lax.broadcasted_iota(jnp.int32, (LANES - HEAD_V, tb), 0) == 0, 1.0, 0.0).astype(_BF16)
    halves = [t[HEAD_V:, :]] if upper_half_only else [t[:HEAD_V, :], t[HEAD_V:, :]]
    for i, half in enumerate(halves):
        vt_ref[first_head + i, 0:HEAD_V, :] = half.astype(_BF16)
        vt_ref[first_head + i, HEAD_V:LANES, :] = ones_pad


def _proj_kernel(x_ref, sh_ref, sc_ref, g_ref, win_ref, qn_ref, kvn_ref, wuq_ref, wukv_ref,
                 gq_ref, gk_ref, seg_ref, cq_ref, saq_ref, sbq_ref, c32_ref, sa32_ref, sb32_ref,
                 c64_ref, sa64_ref, sb64_ref,
                 mq_ref, mk_ref, mvt_ref, na_ref, navt_ref, df_ref, dvt_ref, gqa_ref, gvt_ref):
    h = _norm_mod(x_ref[...], g_ref[...], sh_ref[0], sc_ref[0])
    p = _dot(h.astype(_BF16), win_ref[...])

    qc = p[:, COL_QC:COL_QC + MLA_Q_RANK]
    qn = qc * lax.rsqrt(jnp.mean(qc * qc, axis=-1, keepdims=True) + EPS) * qn_ref[...]
    q = _dot(qn.astype(_BF16), wuq_ref[...])
    mq_ref[...] = (_rope(q, cq_ref[...], saq_ref[...], sbq_ref[...], MLA_ROPE // 4) * MLA_QK ** -0.5).astype(_BF16)
    kvc = p[:, COL_KVC:COL_KVC + MLA_KV_RANK]
    kvn = kvc * lax.rsqrt(jnp.mean(kvc * kvc, axis=-1, keepdims=True) + EPS) * kvn_ref[...]
    kv = _dot(kvn.astype(_BF16), wukv_ref[...])
    kr = _rope(p[:, COL_KR:COL_KR + LANES], c32_ref[:, :LANES], sa32_ref[:, :LANES], sb32_ref[:, :LANES],
               MLA_ROPE // 4)
    lane = lax.broadcasted_iota(jnp.int32, kv.shape, 1)
    mk_ref[...] = jnp.where(lane % LANES < MLA_NOPE, kv, jnp.concatenate([kr] * MLA_HEADS, axis=1)).astype(_BF16)
    for hd in range(MLA_HEADS):
        _store_values_t(mvt_ref, hd, kv[:, LANES * hd:LANES * (hd + 1)], True)

    na_ref[:, 0:256] = (p[:, COL_NA:COL_NA + 256] * NA_HEAD_DIM ** -0.5).astype(_BF16)
    na_ref[:, 256:768] = p[:, COL_NA + 256:COL_NA + 768].astype(_BF16)
    for pr in range(NA_HEADS // 2):
        _store_values_t(navt_ref, 2 * pr, p[:, COL_NA + 512 + LANES * pr:COL_NA + 512 + LANES * (pr + 1)], False)

    dq = p[:, COL_DIFF:COL_DIFF + 256]
    dk = p[:, COL_DIFF + 256:COL_DIFF + 512]
    c32, sa32, sb32 = c32_ref[...], sa32_ref[...], sb32_ref[...]
    df_ref[:, 0:256] = (_rope(dq, c32, sa32, sb32, DIFF_HALF // 4) * DIFF_HALF ** -0.5).astype(_BF16)
    df_ref[:, 256:512] = _rope(dk, c32, sa32, sb32, DIFF_HALF // 4).astype(_BF16)
    for pr in range(DIFF_HEADS // 2):
        _store_values_t(dvt_ref, 2 * pr, p[:, COL_DIFF + 512 + LANES * pr:COL_DIFF + 512 + LANES * (pr + 1)], False)

    gq = p[:, COL_GQA:COL_GQA + 256]
    gk = p[:, COL_GQA + 256:COL_GQA + 384]
    gqn = gq * lax.rsqrt(_seg_mean_sq(gq, seg_ref, GQA_HEAD_DIM) + EPS) * gq_ref[...]
    gk2 = jnp.concatenate([gk, gk], axis=-1)
    gkn = gk2 * lax.rsqrt(_seg_mean_sq(gk2, seg_ref, GQA_HEAD_DIM) + EPS) * gk_ref[...]
    c64, sa64, sb64 = c64_ref[...], sa64_ref[...], sb64_ref[...]
    gqa_ref[:, 0:256] = (_rope(gqn, c64, sa64, sb64, GQA_HEAD_DIM // 4) * GQA_HEAD_DIM ** -0.5).astype(_BF16)
    gqa_ref[:, 256:384] = _rope(gkn, c64, sa64, sb64, GQA_HEAD_DIM // 4)[:, :128].astype(_BF16)
    _store_values_t(gvt_ref, 0, p[:, COL_GQA + 384:COL_GQA + 512], False)


def _proj_prep(xa, mod_l, g_norm, win, qn, kvn, wuq, wukv, gq, gk, seg, tabs, n_lat_blocks, blocks_per_batch,
               n_batch, pos_blocks):
    nt, d = xa.shape
    tb = TOK_BLOCK

    def mrow(i):
        return jnp.minimum(i // blocks_per_batch, n_batch)

    def trow(i):
        return jnp.where(i < n_lat_blocks, i % pos_blocks, pos_blocks)

    full = lambda a: pl.BlockSpec(a.shape, lambda i: (0,) * a.ndim)
    tab_specs = [pl.BlockSpec((tb, t.shape[1]), lambda i: (trow(i), 0)) for t in tabs]
    rows = lambda w: (pl.BlockSpec((tb, w), lambda i: (i, 0)), jax.ShapeDtypeStruct((nt, w), _BF16))
    cols = lambda nh: (pl.BlockSpec((nh, LANES, tb), lambda i: (0, 0, i)), jax.ShapeDtypeStruct((nh, LANES, nt), _BF16))
    outs = [rows(MLA_HEADS * LANES), rows(MLA_HEADS * LANES), cols(MLA_HEADS), rows(768), cols(NA_HEADS),
            rows(512), cols(DIFF_HEADS), rows(384), cols(GQA_KV_HEADS)]
    return pl.pallas_call(
        _proj_kernel,
        grid=(nt // tb,),
        in_specs=[
            pl.BlockSpec((tb, d), lambda i: (i, 0)),
            pl.BlockSpec((1, 1, d), lambda i: (mrow(i), 0, 0)),
            pl.BlockSpec((1, 1, d), lambda i: (mrow(i), 0, 1)),
            full(g_norm), full(win), full(qn), full(kvn), full(wuq), full(wukv), full(gq), full(gk), full(seg),
        ] + tab_specs,
        out_specs=[o[0] for o in outs],
        out_shape=[o[1] for o in outs],
        compiler_params=_cparams(("arbitrary",)),
        name="mix_in_proj",
    )(xa, mod_l, mod_l, g_norm, win, qn, kvn, wuq, wukv, gq, gk, seg, *tabs)


def _softmax_pv(q, k, vt, kc, vct):
    sc = lax.dot_general(kc, q, _NT, preferred_element_type=_F32)
    m = jnp.max(sc, axis=0, keepdims=True)
    if k is not None:
        s = lax.dot_general(k, q, _NT, preferred_element_type=_F32)
        m = jnp.maximum(m, jnp.max(s, axis=0, keepdims=True))
    acc = _dot(vct, jnp.exp((sc - m).astype(_BF16)))
    if k is not None:
        acc = acc + _dot(vt, jnp.exp((s - m).astype(_BF16)))
    return acc[:HEAD_V, :] / acc[HEAD_V:HEAD_V + 1, :]


def _head_operands(q_ref, k_ref, kc_ref, half, padded_heads, use_latent):
    if padded_heads:
        cols = slice(LANES * half, LANES * (half + 1))
        return q_ref[:, cols], (k_ref[:, cols] if use_latent else None), kc_ref[:, cols]
    q = q_ref[...]
    mine = (lax.broadcasted_iota(jnp.int32, q.shape, 1) // HEAD_V) == half
    return jnp.where(mine, q, jnp.zeros_like(q)), (k_ref[...] if use_latent else None), kc_ref[...]


def _attn_kernel(q_ref, k_ref, vt_ref, kc_ref, vct_ref, o_ref, *, use_latent, padded_heads):
    outs = []
    for half in range(2):
        q, k, kc = _head_operands(q_ref, k_ref, kc_ref, half, padded_heads, use_latent)
        outs.append(_softmax_pv(q, k, vt_ref[half], kc, vct_ref[half]))
    o_ref[...] = jnp.concatenate(outs, axis=0).T.astype(o_ref.dtype)


def _diff_kernel(q_ref, k_ref, vt_ref, kc_ref, vct_ref, lam_ref, sub_ref, o_ref, *, use_latent, lam_init):
    lam = (jnp.exp(jnp.sum(lam_ref[0:1, :] * lam_ref[1:2, :], axis=-1, keepdims=True))
           - jnp.exp(jnp.sum(lam_ref[2:3, :] * lam_ref[3:4, :], axis=-1, keepdims=True)) + lam_init)
    q = q_ref[...]
    k = k_ref[...] if use_latent else None
    kc = kc_ref[...]
    part = lax.broadcasted_iota(jnp.int32, q.shape, 1) // DIFF_HALF
    zero = jnp.zeros_like(q)
    outs = []
    for half in range(2):
        o1 = _softmax_pv(jnp.where(part == 2 * half, q, zero), k, vt_ref[half], kc, vct_ref[half])
        o2 = _softmax_pv(jnp.where(part == 2 * half + 1, q, zero), k, vt_ref[half], kc, vct_ref[half])
        o = o1 - lam * o2
        y = o * lax.rsqrt(jnp.mean(o * o, axis=0, keepdims=True) + EPS) * sub_ref[...]
        outs.append(y * (1.0 - lam_init))
    o_ref[...] = jnp.concatenate(outs, axis=0).T.astype(o_ref.dtype)


def _attention(src, k_src, vt_all, geom, tq, use_latent, *, q_col, k_col, kv_pairs, padded_heads=False,
               diff=None):
    bsz, seq, clen, n_lat = geom
    n_pairs = 2
    w = 2 * LANES if padded_heads else LANES
    lq = seq if use_latent else clen
    ctx0 = n_lat // clen
    q0 = 0 if use_latent else n_lat // tq
    kp = (lambda p: p) if kv_pairs > 1 else (lambda p: 0)
    in_specs = [
        pl.BlockSpec((tq, w), lambda b, p, i: (q0 + b * (lq // tq) + i, q_col + p)),
        pl.BlockSpec((seq, w), lambda b, p, i: (b, k_col + kp(p))),
        pl.BlockSpec((2, LANES, seq), lambda b, p, i: (kp(p), 0, b)),
        pl.BlockSpec((clen, w), lambda b, p, i: (ctx0 + b, k_col + kp(p))),
        pl.BlockSpec((2, LANES, clen), lambda b, p, i: (kp(p), 0, ctx0 + b)),
    ]
    args = [src, k_src, vt_all, k_src, vt_all]
    if diff:
        lam_vecs, sub_col, lam_init = diff
        body = functools.partial(_diff_kernel, use_latent=use_latent, lam_init=lam_init)
        in_specs += [pl.BlockSpec(lam_vecs.shape, lambda b, p, i: (0, 0)),
                     pl.BlockSpec(sub_col.shape, lambda b, p, i: (0, 0))]
        args += [lam_vecs, sub_col]
    else:
        body = functools.partial(_attn_kernel, use_latent=use_latent, padded_heads=padded_heads)
    return pl.pallas_call(
        body,
        grid=(bsz, n_pairs, lq // tq),
        in_specs=in_specs,
        out_specs=pl.BlockSpec((tq, LANES), lambda b, p, i: (b * (lq // tq) + i, p)),
        out_shape=jax.ShapeDtypeStruct((bsz * lq, n_pairs * LANES), _BF16),
        compiler_params=_cparams(("arbitrary", "arbitrary", "arbitrary")),
        name="diff_attention" if diff else "softmax_attention",
    )(*args)


def _na_kernel(q_ref, k_ref, v_ref, kc_ref, vc_ref, bias_ref, o_ref, *, n_rows, kr, rows_per_step):
    kc, vc = kc_ref[...], vc_ref[...]
    for i in range(rows_per_step):
        r = pl.program_id(2) * rows_per_step + i
        rs = jnp.clip(r - kr // 2, 0, n_rows - kr)
        start = pl.multiple_of(rs * GRID_W, GRID_W)
        q = q_ref[i * GRID_W:(i + 1) * GRID_W, :]
        k_win = k_ref[pl.ds(start, kr * GRID_W), :]
        v_win = v_ref[pl.ds(start, kr * GRID_W), :]
        lane = lax.broadcasted_iota(jnp.int32, q.shape, 1)
        outs = []
        for half in range(2):
            qh = jnp.where(lane // HEAD_V == half, q, jnp.zeros_like(q))
            s_loc = lax.dot_general(qh, k_win, _NT, preferred_element_type=_F32) + bias_ref[half, r - rs]
            s_ctx = lax.dot_general(qh, kc, _NT, preferred_element_type=_F32)
            m = jnp.maximum(jnp.max(s_loc, axis=-1, keepdims=True), jnp.max(s_ctx, axis=-1, keepdims=True))
            e_loc = jnp.exp((s_loc - m).astype(_BF16))
            e_ctx = jnp.exp((s_ctx - m).astype(_BF16))
            l = (jnp.sum(e_loc.astype(_F32), axis=-1, keepdims=True)
                 + jnp.sum(e_ctx.astype(_F32), axis=-1, keepdims=True))
            outs.append((_dot(e_loc, v_win) + _dot(e_ctx, vc)) / l)
        o_ref[i * GRID_W:(i + 1) * GRID_W, :] = jnp.where(lane // HEAD_V == 0, outs[0], outs[1]).astype(o_ref.dtype)


def _na_attention(src, geom, bias):
    bsz, seq, clen, n_lat = geom
    n_rows = seq // GRID_W
    kr = min(NA_WIN_ROWS, n_rows)
    rows_per_step = math.gcd(n_rows, NA_ROWS_PER_STEP)
    tq = rows_per_step * GRID_W
    ctx0 = n_lat // clen
    n_pairs = NA_HEADS // 2
    return pl.pallas_call(
        functools.partial(_na_kernel, n_rows=n_rows, kr=kr, rows_per_step=rows_per_step),
        grid=(bsz, n_pairs, n_rows // rows_per_step),
        in_specs=[
            pl.BlockSpec((tq, LANES), lambda b, p, r: (b * (seq // tq) + r, p)),
            pl.BlockSpec((seq, LANES), lambda b, p, r: (b, n_pairs + p)),
            pl.BlockSpec((seq, LANES), lambda b, p, r: (b, 2 * n_pairs + p)),
            pl.BlockSpec((clen, LANES), lambda b, p, r: (ctx0 + b, n_pairs + p)),
            pl.BlockSpec((clen, LANES), lambda b, p, r: (ctx0 + b, 2 * n_pairs + p)),
            pl.BlockSpec((2,) + bias.shape[1:], lambda b, p, r: (p, 0, 0, 0)),
        ],
        out_specs=pl.BlockSpec((tq, LANES), lambda b, p, r: (b * (seq // tq) + r, p)),
        out_shape=jax.ShapeDtypeStruct((n_lat, n_pairs * LANES), _BF16),
        compiler_params=_cparams(("arbitrary", "arbitrary", "arbitrary")),
        name="neighbourhood_attention",
    )(src, src, src, src, src, bias)


def _na_bias_table(rpb, n_rows):
    kr = min(NA_WIN_ROWS, n_rows)
    nc = NA_WIN_COLS
    col_q = np.arange(GRID_W)
    col_start = np.clip(col_q - nc // 2, 0, GRID_W - nc)
    key_c = np.arange(GRID_W)
    inside = (key_c[None, :] >= col_start[:, None]) & (key_c[None, :] < col_start[:, None] + nc)
    col_off = key_c[None, :] - col_q[:, None] + (NA_WIN_COLS - 1)
    col_sel = (inside[:, :, None] & (col_off[:, :, None] == np.arange(2 * NA_WIN_COLS - 1))).astype(np.float32)
    row_off = np.arange(kr)[None, :] - np.arange(kr)[:, None] + (NA_WIN_ROWS - 1)
    row_sel = (row_off[:, :, None] == np.arange(2 * NA_WIN_ROWS - 1)).astype(np.float32)
    tab = jnp.einsum('hrj,dir,cxj->hdcix', rpb, row_sel, col_sel, precision=lax.Precision.HIGHEST)
    tab = jnp.where(inside[None, None, :, None, :], tab, -1e30)
    return tab.reshape(rpb.shape[0], kr, GRID_W, kr * GRID_W).astype(_F32)


def _merge_kernel(x_ref, sh_ref, sc_ref, g1_ref, g_ref, oa_ref, ob_ref, oc_ref, od_ref, wg_ref, bg_ref, wb_ref, wo_ref,
                  y_ref):
    x = x_ref[...]
    hb = _norm_mod(x, g_ref[...], sh_ref[0], sc_ref[0]).astype(_BF16)
    merged = None
    for i, o_ref in enumerate((oa_ref, ob_ref, oc_ref, od_ref)):
        gate = 1.0 / (1.0 + jnp.exp(-(_dot(hb, wg_ref[i]) + bg_ref[i])))
        term = gate * _dot(o_ref[...], wb_ref[i])
        merged = term if merged is None else merged + term
    y_ref[...] = x + g1_ref[0] * _dot(merged.astype(_BF16), wo_ref[...])


def _merge(xa, mod_l, g_norm, outs, wg, bg, wb, wo, blocks_per_batch, n_batch):
    nt, d = xa.shape
    tb = TOK_BLOCK

    def mrow(i):
        return jnp.minimum(i // blocks_per_batch, n_batch)

    full = lambda a: pl.BlockSpec(a.shape, lambda i: (0,) * a.ndim)
    return pl.pallas_call(
        _merge_kernel,
        grid=(nt // tb,),
        in_specs=[
            pl.BlockSpec((tb, d), lambda i: (i, 0)),
            pl.BlockSpec((1, 1, d), lambda i: (mrow(i), 0, 0)),
            pl.BlockSpec((1, 1, d), lambda i: (mrow(i), 0, 1)),
            pl.BlockSpec((1, 1, d), lambda i: (mrow(i), 0, 2)),
            full(g_norm),
        ] + [pl.BlockSpec((tb, o.shape[1]), lambda i: (i, 0)) for o in outs] + [
            full(wg), full(bg), full(wb), full(wo),
        ],
        out_specs=pl.BlockSpec((tb, d), lambda i: (i, 0)),
        out_shape=jax.ShapeDtypeStruct((nt, d), _F32),
        compiler_params=_cparams(("arbitrary",)),
        name="branch_merge",
    )(xa, mod_l, mod_l, mod_l, g_norm, *outs, wg, bg, wb, wo)


def _top_rows(s, payload, k):
    n = s.shape[0]
    iota = lax.broadcasted_iota(jnp.int32, s.shape, 0).astype(_F32)
    vals, pays = [], []
    for _ in range(k):
        m = jnp.max(s, axis=0, keepdims=True)
        pos = jnp.min(jnp.where(s == m, iota, float(n)), axis=0, keepdims=True)
        hit = iota == pos
        vals.append(m)
        if payload is None:
            pays.append(pos)
        else:
            pays.append(jnp.max(jnp.where(hit, payload, -1.0), axis=0, keepdims=True))
        s = jnp.where(hit, -jnp.inf, s)
    return jnp.concatenate(vals, axis=0), jnp.concatenate(pays, axis=0)


def _candidate_grid(s1, s2, combine):
    k = PEER_TOPK
    blocks = [combine(s1[0:1, :], s2)]
    blocks += [combine(s1[a:a + 1, :], s2[0:k // 2, :]) for a in range(1, k // 2)]
    blocks.append(combine(s1[k // 2:k, :], s2[0:1, :]))
    return jnp.concatenate(blocks, axis=0)


def _route_kernel(x_ref, sh_ref, sc_ref, g_ref, wqh_ref, wql_ref, skh_ref, skl_ref,
                  h_ref, e_ref, gate_ref, q_s, top_s, idx_s):
    h = _norm_mod(x_ref[...], g_ref[...], sh_ref[0], sc_ref[0])
    h_ref[...] = h.astype(_BF16)
    h_hi, h_lo = _split_bf16(h)
    q = _dot(h_hi, wqh_ref[...]) + _dot(h_hi, wql_ref[...]) + _dot(h_lo, wqh_ref[...])
    for hp in range(2 * PEER_HEADS):
        q_s[hp] = q[:, LANES * hp:LANES * (hp + 1)]

    def sub_scores(hp, c):
        q_hi, q_lo = _split_bf16(q_s[hp])
        sk_hi, sk_lo = skh_ref[hp], skl_ref[hp]
        s = (lax.dot_general(sk_hi, q_hi, _NT, preferred_element_type=_F32)
             + lax.dot_general(sk_hi, q_lo, _NT, preferred_element_type=_F32)
             + lax.dot_general(sk_lo, q_hi, _NT, preferred_element_type=_F32))
        vals, idx = _top_rows(s, None, PEER_TOPK)
        top_s[hp] = vals
        idx_s[hp] = idx
        return c

    lax.fori_loop(0, 2 * PEER_HEADS, sub_scores, 0)

    def head(hd, c):
        cand = _candidate_grid(top_s[2 * hd], top_s[2 * hd + 1], lambda a, b: a + b)
        cidx = _candidate_grid(idx_s[2 * hd], idx_s[2 * hd + 1], lambda a, b: a * float(PEER_NKEYS) + b)
        best, experts = _top_rows(cand, cidx, PEER_TOPK)
        ex = jnp.exp(best - best[0:1, :])
        gate_ref[hd] = ex / jnp.sum(ex, axis=0, keepdims=True)
        e_ref[hd] = experts.astype(jnp.int32)
        return c

    lax.fori_loop(0, PEER_HEADS, head, 0)


def _peer_route(xa, mod_l, g_norm, wq_hi, wq_lo, sk_hi, sk_lo, tb, seq, n_batch):
    nt, d = xa.shape

    def mrow(i):
        return jnp.minimum(i * tb // seq, n_batch)

    full = lambda a: pl.BlockSpec(a.shape, lambda i: (0,) * a.ndim)
    return pl.pallas_call(
        _route_kernel,
        grid=(nt // tb,),
        in_specs=[
            pl.BlockSpec((tb, d), lambda i: (i, 0)),
            pl.BlockSpec((1, 1, d), lambda i: (mrow(i), 0, 3)),
            pl.BlockSpec((1, 1, d), lambda i: (mrow(i), 0, 4)),
            full(g_norm), full(wq_hi), full(wq_lo), full(sk_hi), full(sk_lo),
        ],
        out_specs=[
            pl.BlockSpec((tb, d), lambda i: (i, 0)),
            pl.BlockSpec((PEER_HEADS, PEER_TOPK, tb), lambda i: (0, 0, i)),
            pl.BlockSpec((PEER_HEADS, PEER_TOPK, tb), lambda i: (0, 0, i)),
        ],
        out_shape=[
            jax.ShapeDtypeStruct((nt, d), _BF16),
            jax.ShapeDtypeStruct((PEER_HEADS, PEER_TOPK, nt), jnp.int32),
            jax.ShapeDtypeStruct((PEER_HEADS, PEER_TOPK, nt), _F32),
        ],
        scratch_shapes=[
            pltpu.VMEM((2 * PEER_HEADS, tb, LANES), _F32),
            pltpu.VMEM((2 * PEER_HEADS, PEER_TOPK, tb), _F32),
            pltpu.VMEM((2 * PEER_HEADS, PEER_TOPK, tb), _F32),
        ],
        compiler_params=_cparams(("arbitrary",)),
        name="peer_route",
    )(xa, mod_l, mod_l, g_norm, wq_hi, wq_lo, sk_hi, sk_lo)


def _gather_token(idx_ref, tab_ref, tile_ref, t):
    dst = t * TOK_ROWS
    if not isinstance(t, int):
        dst = pl.multiple_of(dst, TOK_ROWS)
    for p in range(PEER_PAIRS):
        row = pl.multiple_of(idx_ref[t * PEER_PAIRS + p], PACK_ROWS)
        tile_ref[pl.ds(dst + PACK_ROWS * p, PACK_ROWS), :] = tab_ref[pl.ds(row, PACK_ROWS), :]


def _first_block_gather(idx_ref, tab_ref, tile_ref, tb):
    @pl.when(pl.program_id(0) == 0)
    def _():
        def body(t, c):
            _gather_token(idx_ref, tab_ref, tile_ref, t)
            return c

        lax.fori_loop(0, tb, body, 0)


def _ping_pong(fn, bufs_a, bufs_b):
    parity = pl.program_id(0) % 2

    @pl.when(parity == 0)
    def _():
        fn(*bufs_a, *bufs_b)

    @pl.when(parity == 1)
    def _():
        fn(*bufs_b, *bufs_a)


def _unpack_f32(w):
    lo = pltpu.bitcast(w << 16, _F32)
    hi = pltpu.bitcast(w & jnp.uint32(0xFFFF0000), _F32)
    return lo, hi


def _peer_u_kernel(idx_ref, idx_next_ref, h_ref, gate_ref, tab_ref, w_ref, tile_a, tile_b):
    tb = h_ref.shape[0]
    _first_block_gather(idx_ref, tab_ref, tile_a, tb)
    lane = lax.broadcasted_iota(jnp.int32, (PEER_PAIRS, tb), 1)

    def run(src, dst):
        h = h_ref[...]
        a = jnp.zeros((PEER_PAIRS, tb), _F32)
        for g in range(tb // U_GROUP):
            t0 = g * U_GROUP
            los, his = [], []
            for j in range(PACK_ROWS):
                lo, hi = _unpack_f32(src[pl.ds(t0 * TOK_ROWS + j, U_GROUP * PEER_PAIRS, stride=PACK_ROWS), :])
                los.append(lo.astype(_BF16))
                his.append(hi.astype(_BF16))
            rows = jnp.concatenate(los + his, axis=1)
            acc = lax.dot_general(rows, h, _NT, preferred_element_type=_F32)
            for k in range(U_GROUP):
                a = a + jnp.where(lane == t0 + k, acc[PEER_PAIRS * k:PEER_PAIRS * (k + 1), :], 0.0)
            for k in range(U_GROUP):
                _gather_token(idx_next_ref, tab_ref, dst, t0 + k)
        act = 0.5 * a * (1.0 + lax.erf(a * (2.0 ** -0.5)))
        w_ref[0] = gate_ref[0] * act

    _ping_pong(run, (tile_a,), (tile_b,))


def _peer_v_kernel(idx_ref, idx_next_ref, w_ref, w_next_ref, x_ref, g2_ref, pe_ref, po_ref, tab_ref, y_ref,
                   tile_a, tile_b, lhs_a, lhs_b):
    tb = x_ref.shape[0]
    _first_block_gather(idx_ref, tab_ref, tile_a, tb)

    def build_lhs(wblk_ref, lhs_ref):
        w_hi, w_lo = _split_bf16(wblk_ref[0].T)
        v = 0
        for spread_ref in (pe_ref, po_ref):
            for part in (w_hi, w_lo):
                val = _dot(part, spread_ref[...])
                for hf in range(2):
                    lhs_ref[hf, pl.ds(v, tb, stride=SUBLANES), :] = val[:, LANES * hf:LANES * (hf + 1)]
                v += 1

    @pl.when(pl.program_id(0) == 0)
    def _():
        lhs_a[...] = jnp.zeros(lhs_a.shape, lhs_a.dtype)
        lhs_b[...] = jnp.zeros(lhs_b.shape, lhs_b.dtype)
        build_lhs(w_ref, lhs_a)

    sub = lax.broadcasted_iota(jnp.int32, (SUBLANES, LANES), 0)

    def run(src, lhs_ref, dst, lhs_next):
        build_lhs(w_next_ref, lhs_next)
        g2 = g2_ref[0]
        for g in range(tb // SUBLANES):
            grp = [jnp.zeros((SUBLANES, LANES), _F32) for _ in range(2 * PACK_ROWS)]
            for k in range(SUBLANES):
                t = g * SUBLANES + k
                lhs = jnp.concatenate([lhs_ref[0, pl.ds(t * SUBLANES, SUBLANES), :],
                                       lhs_ref[1, pl.ds(t * SUBLANES, SUBLANES), :]], axis=1).astype(_BF16)
                for j in range(PACK_ROWS):
                    rows = pltpu.bitcast(src[pl.ds(t * TOK_ROWS + j, PEER_PAIRS, stride=PACK_ROWS), :], _BF16)
                    res = _dot(lhs, rows)
                    lo = res[0:1, :] + res[1:2, :]
                    hi = res[2:3, :] + res[3:4, :]
                    grp[j] = jnp.where(sub == k, lo, grp[j])
                    grp[PACK_ROWS + j] = jnp.where(sub == k, hi, grp[PACK_ROWS + j])
                _gather_token(idx_next_ref, tab_ref, dst, t)
            r0 = g * SUBLANES
            y_ref[r0:r0 + SUBLANES, :] = x_ref[r0:r0 + SUBLANES, :] + g2 * jnp.concatenate(grp, axis=1)

    _ping_pong(run, (tile_a, lhs_a), (tile_b, lhs_b))


def _table_spec(tab):
    return pl.BlockSpec(tab.shape, lambda i: (0, 0), pipeline_mode=pl.Buffered(1))


def _pair_specs(n_blocks):
    n_pairs = PEER_TB * PEER_PAIRS
    return [pl.BlockSpec((n_pairs,), lambda i: (i,), memory_space=pltpu.SMEM),
            pl.BlockSpec((n_pairs,), lambda i: (jnp.minimum(i + 1, n_blocks - 1),), memory_space=pltpu.SMEM)]


def _tile_scratch():
    return [pltpu.VMEM((PEER_TB * TOK_ROWS, LANES), jnp.uint32)] * 2


def _peer_u(idx, h, gate3, tab_u):
    nt, d = h.shape
    tb = PEER_TB
    return pl.pallas_call(
        _peer_u_kernel,
        grid=(nt // tb,),
        in_specs=_pair_specs(nt // tb) + [
            pl.BlockSpec((tb, d), lambda i: (i, 0)),
            pl.BlockSpec((1, PEER_PAIRS, tb), lambda i: (i, 0, 0)),
            _table_spec(tab_u),
        ],
        out_specs=pl.BlockSpec((1, PEER_PAIRS, tb), lambda i: (i, 0, 0)),
        out_shape=jax.ShapeDtypeStruct((nt // tb, PEER_PAIRS, tb), _F32),
        scratch_shapes=_tile_scratch(),
        compiler_params=_cparams(("arbitrary",)),
        name="peer_expert_in",
    )(idx, idx, h, gate3, tab_u)


def _peer_v(idx, w3, xa, mod_l, spread_even, spread_odd, tab_v, seq, n_batch):
    nt, d = xa.shape
    tb = PEER_TB

    def mrow(i):
        return jnp.minimum(i * tb // seq, n_batch)

    full = lambda a: pl.BlockSpec(a.shape, lambda i: (0,) * a.ndim)
    return pl.pallas_call(
        _peer_v_kernel,
        grid=(nt // tb,),
        in_specs=_pair_specs(nt // tb) + [
            pl.BlockSpec((1, PEER_PAIRS, tb), lambda i: (i, 0, 0)),
            pl.BlockSpec((1, PEER_PAIRS, tb), lambda i: (jnp.minimum(i + 1, nt // tb - 1), 0, 0)),
            pl.BlockSpec((tb, d), lambda i: (i, 0)),
            pl.BlockSpec((1, 1, d), lambda i: (mrow(i), 0, 5)),
            full(spread_even), full(spread_odd),
            _table_spec(tab_v),
        ],
        out_specs=pl.BlockSpec((tb, d), lambda i: (i, 0)),
        out_shape=jax.ShapeDtypeStruct((nt, d), _F32),
        scratch_shapes=_tile_scratch() + [pltpu.VMEM((2, tb * SUBLANES, LANES), _F32)] * 2,
        compiler_params=_cparams(("arbitrary",)),
        name="peer_expert_out",
    )(idx, idx, w3, w3, xa, mod_l, spread_even, spread_odd, tab_v)


def _final_kernel(x_ref, g_ref, o_ref):
    x = x_ref[...]
    o_ref[...] = x * lax.rsqrt(jnp.mean(x * x, axis=-1, keepdims=True) + EPS) * g_ref[...]


def _final_norm(x, g):
    n, d = x.shape
    tb = TOK_BLOCK
    return pl.pallas_call(
        _final_kernel,
        grid=(n // tb,),
        in_specs=[pl.BlockSpec((tb, d), lambda i: (i, 0)), pl.BlockSpec((1, d), lambda i: (0, 0))],
        out_specs=pl.BlockSpec((tb, d), lambda i: (i, 0)),
        out_shape=jax.ShapeDtypeStruct((n, d), _F32),
        compiler_params=_cparams(("arbitrary",)),
        name="final_norm",
    )(x, g)


def _rope_tables(seq, dr):
    pos = jnp.arange(seq)
    rows = (pos // GRID_W).astype(_F32)
    cols = (pos % GRID_W).astype(_F32)
    h = dr // 2
    inv_freq = jnp.power(ROPE_BASE, -jnp.arange(0, h, 2, dtype=_F32) / h)
    ang_r = rows[:, None] * inv_freq[None, :]
    ang_c = cols[:, None] * inv_freq[None, :]
    cr, sr, cc, sc = jnp.cos(ang_r), jnp.sin(ang_r), jnp.cos(ang_c), jnp.sin(ang_c)
    z = jnp.zeros_like(sr)
    cos = jnp.concatenate([cr, cr, cc, cc], axis=-1)
    sa = jnp.concatenate([-sr, z, -sc, z], axis=-1)
    sb = jnp.concatenate([z, sr, z, sc], axis=-1)
    pad1 = jnp.ones((TOK_BLOCK, dr), _F32)
    pad0 = jnp.zeros((TOK_BLOCK, dr), _F32)
    return (jnp.concatenate([cos, pad1], axis=0), jnp.concatenate([sa, pad0], axis=0),
            jnp.concatenate([sb, pad0], axis=0))


def _pack_table(t):
    e, d = t.shape
    bits = lax.bitcast_convert_type(t.astype(_BF16), jnp.uint16).astype(jnp.uint32)
    packed = bits[:, :d // 2] | (bits[:, d // 2:] << 16)
    return packed.reshape(e * PACK_ROWS, LANES)


def _reorder_w_in(w_in):
    d = w_in.shape[0]
    gq0 = 416 + 768 + 768
    gq = w_in[:, gq0:gq0 + 256].reshape(d, GQA_Q_HEADS, GQA_HEAD_DIM)[:, jnp.array(GQA_Q_ORDER), :].reshape(d, 256)
    zeros = lambda n: jnp.zeros((d, n), w_in.dtype)
    return jnp.concatenate([w_in[:, :384], w_in[:, 416:gq0], gq, w_in[:, gq0 + 256:],
                            zeros(MLA_NOPE), w_in[:, 384:416], zeros(LANES - MLA_NOPE - MLA_ROPE)], axis=1)


def _pad_heads(w, n_heads, width):
    k = w.shape[0]
    w = w.reshape(k, n_heads, width)
    return jnp.concatenate([w, jnp.zeros((k, n_heads, LANES - width), w.dtype)], axis=-1).reshape(k, n_heads * LANES)


def kernel(x, c, ctx, c_ctx, w_mod, b_mod, norm_mix, norm_ffn, w_in, mla_q_norm, mla_kv_norm, mla_w_uq,
           mla_w_ukv, na_rpb, diff_lam_q1, diff_lam_k1, diff_lam_q2, diff_lam_k2, diff_subln, gqa_q_norm,
           gqa_k_norm, w_branch, w_gate, b_gate, w_out, peer_w_q, peer_subkeys, peer_u, peer_v, final_norm):
    bsz, seq, d = x.shape
    clen = ctx.shape[1]
    depth = w_mod.shape[0]
    n_lat, n_ctx = bsz * seq, bsz * clen
    assert d == D_MODEL and seq % TOK_BLOCK == 0 and clen % TOK_BLOCK == 0 and seq % GRID_W == 0
    assert bsz < MOD_ROWS and seq % PEER_TB == 0 and clen % PEER_TB == 0
    blocks_per_batch = seq // TOK_BLOCK
    n_lat_blocks = n_lat // TOK_BLOCK
    tq = math.gcd(QUERY_BLOCK, seq)

    cond = jnp.zeros((MOD_ROWS, d), _F32).at[:bsz].set(c).at[bsz].set(c_ctx)
    mod = _mod_all(cond, w_mod, b_mod)

    c32, sa32, sb32 = _rope_tables(seq, MLA_ROPE)
    c64, sa64, sb64 = _rope_tables(seq, GQA_HEAD_DIM)
    rows_t = seq + TOK_BLOCK
    one64 = jnp.ones((rows_t, MLA_NOPE), _F32)
    zero64 = jnp.zeros((rows_t, MLA_NOPE), _F32)
    one32 = jnp.ones((rows_t, LANES - MLA_QK), _F32)
    zero32 = jnp.zeros((rows_t, LANES - MLA_QK), _F32)
    cq = jnp.tile(jnp.concatenate([one64, c32, one32], axis=1), (1, MLA_HEADS))
    saq = jnp.tile(jnp.concatenate([zero64, sa32, zero32], axis=1), (1, MLA_HEADS))
    sbq = jnp.tile(jnp.concatenate([zero64, sb32, zero32], axis=1), (1, MLA_HEADS))
    tabs = (cq, saq, sbq, jnp.tile(c32, (1, 8)), jnp.tile(sa32, (1, 8)), jnp.tile(sb32, (1, 8)),
            jnp.tile(c64, (1, 4)), jnp.tile(sa64, (1, 4)), jnp.tile(sb64, (1, 4)))
    seg = jnp.asarray(np.kron(np.eye(256 // GQA_HEAD_DIM), np.ones((GQA_HEAD_DIM, GQA_HEAD_DIM))), _BF16)

    pair = np.arange(PEER_PAIRS)[:, None]
    spread_even = jnp.asarray(np.arange(2 * PEER_PAIRS)[None, :] == 2 * pair, _BF16)
    spread_odd = jnp.asarray(np.arange(2 * PEER_PAIRS)[None, :] == 2 * pair + 1, _BF16)

    n_rows = seq // GRID_W
    nt = n_lat + n_ctx
    geom = (bsz, seq, clen, n_lat)
    route_block = ROUTE_BLOCK if (seq % ROUTE_BLOCK == 0 and nt % ROUTE_BLOCK == 0) else TOK_BLOCK
    xa = jnp.concatenate([x.reshape(n_lat, d), ctx.reshape(n_ctx, d)], axis=0)

    for l in range(depth):
        mod_l = mod[l].reshape(MOD_ROWS, 1, N_MOD * d)
        g_mix = norm_mix[l].reshape(1, d)
        mq, mk, mvt, na, navt, df, dvt, gqa, gvt = _proj_prep(
            xa, mod_l, g_mix, _reorder_w_in(w_in[l]).astype(_BF16),
            mla_q_norm[l].reshape(1, -1), mla_kv_norm[l].reshape(1, -1),
            _pad_heads(mla_w_uq[l], MLA_HEADS, MLA_QK).astype(_BF16), mla_w_ukv[l].astype(_BF16),
            jnp.tile(gqa_q_norm[l], GQA_Q_HEADS).reshape(1, -1), jnp.tile(gqa_k_norm[l], 4).reshape(1, -1),
            seg, tabs, n_lat_blocks, blocks_per_batch, bsz, blocks_per_batch)

        def attend(src, k_src, vt_all, **kw):
            return jnp.concatenate([_attention(src, k_src, vt_all, geom, tq, True, **kw),
                                    _attention(src, k_src, vt_all, geom, clen, False, **kw)], axis=0)

        o_a = attend(mq, mk, mvt, q_col=0, k_col=0, kv_pairs=2, padded_heads=True)
        o_b = jnp.concatenate([
            _na_attention(na, geom, _na_bias_table(na_rpb[l], n_rows)),
            _attention(na, na, navt, geom, clen, False, q_col=0, k_col=2, kv_pairs=2)], axis=0)
        lam_vecs = jnp.stack([diff_lam_q1[l], diff_lam_k1[l], diff_lam_q2[l], diff_lam_k2[l]], axis=0)
        lam_init = 0.8 - 0.6 * math.exp(-0.3 * l)
        o_c = attend(df, df, dvt, q_col=0, k_col=2, kv_pairs=2,
                     diff=(lam_vecs, diff_subln[l].reshape(-1, 1), lam_init))
        o_d = attend(gqa, gqa, gvt, q_col=0, k_col=2, kv_pairs=1)

        w_br = w_branch[l].astype(_BF16)
        w_br_d = w_br[3].reshape(GQA_Q_HEADS, GQA_HEAD_DIM, d)[jnp.array(GQA_Q_ORDER)].reshape(-1, d)
        w_br = jnp.concatenate([w_br[:3], w_br_d[None]], axis=0)
        xa = _merge(xa, mod_l, g_mix, (o_a, o_b, o_c, o_d), w_gate[l].astype(_BF16), b_gate[l].reshape(4, 1, d),
                    w_br, w_out[l].astype(_BF16), blocks_per_batch, bsz)

        wq_hi, wq_lo = _split_bf16(peer_w_q[l])
        sk = peer_subkeys[l].reshape(2 * PEER_HEADS, PEER_NKEYS, PEER_QDIM // 2)
        sk_hi, sk_lo = _split_bf16(sk)
        h2, experts, gates = _peer_route(xa, mod_l, norm_ffn[l].reshape(1, d), wq_hi, wq_lo, sk_hi, sk_lo,
                                         route_block, seq, bsz)
        idx = (experts.reshape(PEER_PAIRS, nt).T * PACK_ROWS).reshape(-1)
        gate3 = gates.reshape(PEER_PAIRS, nt // PEER_TB, PEER_TB).transpose(1, 0, 2)
        w3 = _peer_u(idx, h2, gate3, _pack_table(peer_u[l]))
        xa = _peer_v(idx, w3, xa, mod_l, spread_even, spread_odd, _pack_table(peer_v[l]), seq, bsz)

    return _final_norm(xa[:n_lat], final_norm.reshape(1, d)).reshape(bsz, seq, d)
```
